```python
import jax, jax.numpy as jnp
from jax import lax
import numpy as np

D_MODEL = 1024
BATCH = 2
SEQ = 8192
DEPTH = 4
DEC_BATCH = 128
DEC_SEQ = 8
PAST_LEN = 8192
PAGE_SIZE = 128

N_MIXERS = 3
N_LAYERS_A = (DEPTH + 2) // 3
N_LAYERS_B = (DEPTH + 1) // 3
N_LAYERS_C = DEPTH // 3
DN_ALPHA = (2 * DEPTH) ** 0.25
DN_BETA = (8 * DEPTH) ** -0.25
LN_EPS = 1e-5
RWKV_HEAD = 64
RWKV_HEADS = D_MODEL // RWKV_HEAD
DECAY_LORA = 64
AAA_LORA = 64
GATE_LORA = 128
GN_EPS = 64e-5
CONV_WIDTH = 31
MLA_HEADS = 8
QK_NOPE = 128
QK_ROPE = 64
V_HEAD = 128
KV_LORA = 256
Q_LORA = 384
ROPE_THETA = 10000.0
MLA_SCALE = (QK_NOPE + QK_ROPE) ** -0.5
Q_BLOCK = 128
RMS_EPS = 1e-6
D_FF = 2816
FFN_CONV_WIDTH = 3

kernel_name = 'hybrid_rwkv7_conformer_mla_convffn_step'


def layer_norm(x, g, b, eps=LN_EPS):
    xf = x.astype(jnp.float32)
    mu = xf.mean(-1, keepdims=True)
    var = jnp.square(xf - mu).mean(-1, keepdims=True)
    return ((xf - mu) * lax.rsqrt(var + eps) * g + b).astype(x.dtype)


def rms_norm(x, g, eps=RMS_EPS):
    xf = x.astype(jnp.float32)
    return (xf * lax.rsqrt(jnp.mean(xf * xf, -1, keepdims=True) + eps) * g).astype(x.dtype)


def deepnorm(x, h, g, b):
    return layer_norm(DN_ALPHA * x + h, g, b)


def causal_dwconv(hist, w, b):
    out = lax.conv_general_dilated(hist, w[:, None, :].astype(hist.dtype), (1,), 'VALID',
                                   dimension_numbers=('NWC', 'WIO', 'NWC'),
                                   feature_group_count=hist.shape[-1])
    return out + b


def rope(x, pos):
    half = x.shape[-1] // 2
    inv = ROPE_THETA ** (-jnp.arange(half, dtype=jnp.float32) / half)
    ang = pos[:, None] * inv[None, :]
    ang = ang.reshape((ang.shape[0],) + (1,) * (x.ndim - 3) + (half,))
    cos, sin = jnp.cos(ang), jnp.sin(ang)
    xf = x.astype(jnp.float32)
    x1, x2 = xf[..., :half], xf[..., half:]
    return jnp.concatenate([x1 * cos - x2 * sin, x1 * sin + x2 * cos], -1).astype(x.dtype)


def rwkv7_mix(x, shift, wkv, mu, w_rkv, w1, w2, a1, a2, g1, g2, vec, r_k, w_o):
    B, T, D = x.shape
    H, N = RWKV_HEADS, RWKV_HEAD
    x_prev = jnp.concatenate([shift[:, None, :].astype(x.dtype), x[:, :-1]], axis=1)
    xm = x[:, :, None, :] + (x_prev - x)[:, :, None, :] * mu
    rkv = jnp.einsum('btjd,jde->btje', xm[:, :, :3], w_rkv)
    r, k, v = rkv[:, :, 0], rkv[:, :, 1], rkv[:, :, 2]
    xw, xa, xg = xm[:, :, 3], xm[:, :, 4], xm[:, :, 5]
    w0, a0, k_k, k_a, gn_g, gn_b = vec[0], vec[1], vec[2], vec[3], vec[4], vec[5]
    w_pre = (w0 + jnp.tanh(xw @ w1) @ w2).astype(jnp.float32)
    decay = jnp.exp(-jnp.exp(-jax.nn.softplus(-w_pre) - 0.5))
    a = jax.nn.sigmoid((a0 + (xa @ a1) @ a2).astype(jnp.float32))
    g = jax.nn.sigmoid(xg @ g1) @ g2
    hd = lambda t: t.reshape(B, T, H, N)
    kk = hd((k * k_k).astype(jnp.float32))
    kk = kk / jnp.maximum(jnp.sqrt(jnp.sum(kk * kk, -1, keepdims=True)), 1e-12)
    kh = hd(k.astype(jnp.float32) * (1.0 + (a - 1.0) * k_a))
    rh, vh = hd(r.astype(jnp.float32)), hd(v.astype(jnp.float32))
    a_vec, b_vec = -kk, kk * hd(a)

    def step(S, inp):
        r_t, d_t, k_t, v_t, a_t, b_t = inp
        sa = jnp.einsum('bhvk,bhk->bhv', S, a_t)
        S = S * d_t[:, :, None, :] + sa[..., None] * b_t[:, :, None, :] + v_t[..., None] * k_t[:, :, None, :]
        return S, jnp.einsum('bhvk,bhk->bhv', S, r_t)

    tm = lambda t: jnp.moveaxis(t, 1, 0)
    S, y = lax.scan(step, wkv.astype(jnp.float32),
                    (tm(rh), tm(hd(decay)), tm(kh), tm(vh), tm(a_vec), tm(b_vec)))
    y = jnp.moveaxis(y, 0, 1)
    y_mu = y.mean(-1, keepdims=True)
    y_var = jnp.square(y - y_mu).mean(-1, keepdims=True)
    y = ((y - y_mu) * lax.rsqrt(y_var + GN_EPS)).reshape(B, T, D) * gn_g + gn_b
    y = y + (jnp.sum(rh * kh * r_k, -1, keepdims=True) * vh).reshape(B, T, D)
    return (y.astype(x.dtype) * g) @ w_o, S, x[:, -1]


def conformer_conv(x, hist0, w_pw1, b_pw1, w_dw, b_dw, ln, w_pw2, b_pw2):
    h = x @ w_pw1 + b_pw1
    u = h[..., :D_MODEL] * jax.nn.sigmoid(h[..., D_MODEL:])
    hist = jnp.concatenate([hist0.astype(u.dtype), u], axis=1)
    c = jax.nn.silu(layer_norm(causal_dwconv(hist, w_dw, b_dw), ln[0], ln[1]))
    return c @ w_pw2 + b_pw2, hist[:, -(CONV_WIDTH - 1):]


def mla_project(x, pos, w_dq, q_norm, w_uq, w_dkv, kv_norm, w_uk):
    cq = rms_norm(x @ w_dq, q_norm)
    q = jnp.einsum('btr,rhe->bthe', cq, w_uq)
    q_pe = rope(q[..., QK_NOPE:], pos)
    kv = x @ w_dkv
    ckv = rms_norm(kv[..., :KV_LORA], kv_norm)
    kpe = rope(kv[..., KV_LORA:], pos)
    q_lat = jnp.einsum('bthn,lhn->bthl', q[..., :QK_NOPE], w_uk)
    return q_lat, q_pe, ckv, kpe


def latent_scores(q_lat, q_pe, ckv, kpe):
    s = jnp.einsum('bthl,bsl->bhts', q_lat, ckv) + jnp.einsum('bthr,bsr->bhts', q_pe, kpe)
    return s.astype(jnp.float32) * MLA_SCALE


def mla_prompt_attention(q_lat, q_pe, ckv, kpe):
    B, S = q_lat.shape[:2]
    nb = S // Q_BLOCK
    blocks = lambda t: jnp.moveaxis(t.reshape((B, nb, Q_BLOCK) + t.shape[2:]), 1, 0)
    k_pos = jnp.arange(S)

    def one_block(args):
        ql, qp, start = args
        q_pos = start + jnp.arange(Q_BLOCK)
        s = jnp.where(k_pos[None, :] <= q_pos[:, None], latent_scores(ql, qp, ckv, kpe), -jnp.inf)
        p = jax.nn.softmax(s, axis=-1).astype(ckv.dtype)
        return jnp.einsum('bhts,bsl->bthl', p, ckv)

    o = lax.map(one_block, (blocks(q_lat), blocks(q_pe), jnp.arange(nb) * Q_BLOCK))
    return jnp.moveaxis(o, 0, 1).reshape(B, S, MLA_HEADS, KV_LORA)


def mla_sample_attention(q_lat, q_pe, ckv, kpe, pool_ckv, pool_kpe, page_table):
    DB, T = q_lat.shape[:2]
    past_ckv = pool_ckv[page_table].reshape(DB, -1, KV_LORA)
    past_kpe = pool_kpe[page_table].reshape(DB, -1, QK_ROPE)
    P = past_ckv.shape[1]
    causal = jnp.arange(T)[None, :] <= jnp.arange(T)[:, None]
    s = jnp.concatenate([latent_scores(q_lat, q_pe, past_ckv, past_kpe),
                         jnp.where(causal, latent_scores(q_lat, q_pe, ckv, kpe), -jnp.inf)], axis=-1)
    p = jax.nn.softmax(s, axis=-1).astype(ckv.dtype)
    return (jnp.einsum('bhts,bsl->bthl', p[..., :P], past_ckv)
            + jnp.einsum('bhts,bsl->bthl', p[..., P:], ckv))


def mla_output(o_lat, w_uv, w_o):
    o = jnp.einsum('bthl,lhv->bthv', o_lat, w_uv)
    return o.reshape(o.shape[:2] + (-1,)) @ w_o


def conv_ffn(x, hist0, w_in, w_dw, b_dw, w_out):
    h = x @ w_in
    a, b = h[..., :D_FF], h[..., D_FF:]
    hist = jnp.concatenate([hist0.astype(a.dtype), a], axis=1)
    c = causal_dwconv(hist, w_dw, b_dw)
    return (jax.nn.silu(c) * b) @ w_out, hist[:, -(FFN_CONV_WIDTH - 1):]


def setup_inputs(seed: int = 0) -> dict:
    key = jax.random.key(seed)
    ks = iter(jax.random.split(key, 64))
    nrm = lambda shape, scale: jax.random.normal(next(ks), shape, jnp.float32) * scale
    uni = lambda shape, lo, hi: jax.random.uniform(next(ks), shape, jnp.float32, lo, hi)
    D, H, N = D_MODEL, RWKV_HEADS, RWKV_HEAD
    n_pages = PAST_LEN // PAGE_SIZE
    n_used = DEC_BATCH * n_pages
    n_pool = n_used + n_used // 4
    page_table = jax.random.permutation(next(ks), n_pool)[:n_used].reshape(DEC_BATCH, n_pages).astype(jnp.int32)
    rwkv_vec = jnp.stack([uni((N_LAYERS_A, D), -5.0, 1.0),
                          nrm((N_LAYERS_A, D), 0.1),
                          0.85 + nrm((N_LAYERS_A, D), 0.05),
                          1.0 + nrm((N_LAYERS_A, D), 0.05),
                          1.0 + nrm((N_LAYERS_A, D), 0.05),
                          nrm((N_LAYERS_A, D), 0.02)], axis=1)
    conf_ln = jnp.stack([1.0 + nrm((N_LAYERS_B, D), 0.05), nrm((N_LAYERS_B, D), 0.02)], axis=1)
    return {
        'x_prompt': nrm((BATCH, SEQ, D), 1.0),
        'x_sample': nrm((DEC_BATCH, DEC_SEQ, D), 1.0),
        'state_wkv': nrm((N_LAYERS_A, DEC_BATCH, H, N, N), 0.3),
        'state_shift': nrm((N_LAYERS_A, DEC_BATCH, D), 1.0),
        'state_conv': nrm((N_LAYERS_B, DEC_BATCH, CONV_WIDTH - 1, D), 0.5),
        'cache_mla_ckv': nrm((N_LAYERS_C, n_pool, PAGE_SIZE, KV_LORA), 1.0),
        'cache_mla_kpe': nrm((N_LAYERS_C, n_pool, PAGE_SIZE, QK_ROPE), 1.0),
        'state_ffn_conv': nrm((DEPTH, DEC_BATCH, FFN_CONV_WIDTH - 1, D_FF), 1.0),
        'page_table': page_table,
        'ln_g': 1.0 + nrm((DEPTH, 2, D), 0.05),
        'ln_b': nrm((DEPTH, 2, D), 0.02),
        'rwkv_mu': uni((N_LAYERS_A, 6, D), 0.0, 1.0),
        'rwkv_w_rkv': nrm((N_LAYERS_A, 3, D, D), D ** -0.5),
        'rwkv_w1': nrm((N_LAYERS_A, D, DECAY_LORA), D ** -0.5),
        'rwkv_w2': nrm((N_LAYERS_A, DECAY_LORA, D), 0.1 * DECAY_LORA ** -0.5),
        'rwkv_a1': nrm((N_LAYERS_A, D, AAA_LORA), D ** -0.5),
        'rwkv_a2': nrm((N_LAYERS_A, AAA_LORA, D), 0.1 * AAA_LORA ** -0.5),
        'rwkv_g1': nrm((N_LAYERS_A, D, GATE_LORA), D ** -0.5),
        'rwkv_g2': nrm((N_LAYERS_A, GATE_LORA, D), GATE_LORA ** -0.5),
        'rwkv_vec': rwkv_vec,
        'rwkv_r_k': nrm((N_LAYERS_A, H, N), 0.1),
        'rwkv_w_o': nrm((N_LAYERS_A, D, D), DN_BETA * D ** -0.5),
        'conf_w_pw1': nrm((N_LAYERS_B, D, 2 * D), D ** -0.5),
        'conf_b_pw1': nrm((N_LAYERS_B, 2 * D), 0.02),
        'conf_w_dw': nrm((N_LAYERS_B, CONV_WIDTH, D), CONV_WIDTH ** -0.5),
        'conf_b_dw': nrm((N_LAYERS_B, D), 0.02),
        'conf_ln': conf_ln,
        'conf_w_pw2': nrm((N_LAYERS_B, D, D), DN_BETA * D ** -0.5),
        'conf_b_pw2': nrm((N_LAYERS_B, D), 0.02),
        'mla_w_dq': nrm((N_LAYERS_C, D, Q_LORA), D ** -0.5),
        'mla_q_norm': 1.0 + nrm((N_LAYERS_C, Q_LORA), 0.05),
        'mla_w_uq': nrm((N_LAYERS_C, Q_LORA, MLA_HEADS, QK_NOPE + QK_ROPE), Q_LORA ** -0.5),
        'mla_w_dkv': nrm((N_LAYERS_C, D, KV_LORA + QK_ROPE), D ** -0.5),
        'mla_kv_norm': 1.0 + nrm((N_LAYERS_C, KV_LORA), 0.05),
        'mla_w_uk': nrm((N_LAYERS_C, KV_LORA, MLA_HEADS, QK_NOPE), KV_LORA ** -0.5),
        'mla_w_uv': nrm((N_LAYERS_C, KV_LORA, MLA_HEADS, V_HEAD), KV_LORA ** -0.5),
        'mla_w_o': nrm((N_LAYERS_C, MLA_HEADS * V_HEAD, D), DN_BETA * (MLA_HEADS * V_HEAD) ** -0.5),
        'ffn_w_in': nrm((DEPTH, D, 2 * D_FF), D ** -0.5),
        'ffn_w_dw': nrm((DEPTH, FFN_CONV_WIDTH, D_FF), FFN_CONV_WIDTH ** -0.5),
        'ffn_b_dw': nrm((DEPTH, D_FF), 0.02),
        'ffn_w_out': nrm((DEPTH, D_FF, D), DN_BETA * D_FF ** -0.5),
    }


def reference(x_prompt, x_sample, state_wkv, state_shift, state_conv, cache_mla_ckv, cache_mla_kpe,
              state_ffn_conv, page_table, ln_g, ln_b, rwkv_mu, rwkv_w_rkv, rwkv_w1, rwkv_w2, rwkv_a1,
              rwkv_a2, rwkv_g1, rwkv_g2, rwkv_vec, rwkv_r_k, rwkv_w_o, conf_w_pw1, conf_b_pw1, conf_w_dw,
              conf_b_dw, conf_ln, conf_w_pw2, conf_b_pw2, mla_w_dq, mla_q_norm, mla_w_uq, mla_w_dkv,
              mla_kv_norm, mla_w_uk, mla_w_uv, mla_w_o, ffn_w_in, ffn_w_dw, ffn_b_dw, ffn_w_out):
    Bp = x_prompt.shape[0]
    past_len = page_table.shape[1] * cache_mla_ckv.shape[2]
    pos_p = jnp.arange(x_prompt.shape[1], dtype=jnp.float32)
    pos_s = past_len + jnp.arange(x_sample.shape[1], dtype=jnp.float32)
    xp, xs = x_prompt, x_sample
    p_wkv, p_shift, p_conv, p_ckv, p_kpe, p_ffn = [], [], [], [], [], []
    s_wkv, s_shift, s_conv, s_ckv, s_kpe, s_ffn = [], [], [], [], [], []
    for i in range(DEPTH):
        j = i // N_MIXERS
        kind = i % N_MIXERS
        if kind == 0:
            prm = (rwkv_mu[j], rwkv_w_rkv[j], rwkv_w1[j], rwkv_w2[j], rwkv_a1[j], rwkv_a2[j],
                   rwkv_g1[j], rwkv_g2[j], rwkv_vec[j], rwkv_r_k[j], rwkv_w_o[j])
            hp, st, sh = rwkv7_mix(xp, jnp.zeros((Bp, D_MODEL), xp.dtype),
                                   jnp.zeros((Bp, RWKV_HEADS, RWKV_HEAD, RWKV_HEAD), jnp.float32), *prm)
            p_wkv.append(st)
            p_shift.append(sh)
            hs, st, sh = rwkv7_mix(xs, state_shift[j], state_wkv[j], *prm)
            s_wkv.append(st)
            s_shift.append(sh)
        elif kind == 1:
            prm = (conf_w_pw1[j], conf_b_pw1[j], conf_w_dw[j], conf_b_dw[j], conf_ln[j],
                   conf_w_pw2[j], conf_b_pw2[j])
            hp, cb = conformer_conv(xp, jnp.zeros((Bp, CONV_WIDTH - 1, D_MODEL), xp.dtype), *prm)
            p_conv.append(cb)
            hs, cb = conformer_conv(xs, state_conv[j], *prm)
            s_conv.append(cb)
        else:
            proj = (mla_w_dq[j], mla_q_norm[j], mla_w_uq[j], mla_w_dkv[j], mla_kv_norm[j], mla_w_uk[j])
            ql, qp, ckv, kpe = mla_project(xp, pos_p, *proj)
            hp = mla_output(mla_prompt_attention(ql, qp, ckv, kpe), mla_w_uv[j], mla_w_o[j])
            p_ckv.append(ckv)
            p_kpe.append(kpe)
            ql, qp, ckv, kpe = mla_project(xs, pos_s, *proj)
            hs = mla_output(mla_sample_attention(ql, qp, ckv, kpe, cache_mla_ckv[j], cache_mla_kpe[j],
                                                 page_table), mla_w_uv[j], mla_w_o[j])
            s_ckv.append(ckv)
            s_kpe.append(kpe)
        xp = deepnorm(xp, hp, ln_g[i, 0], ln_b[i, 0])
        xs = deepnorm(xs, hs, ln_g[i, 0], ln_b[i, 0])
        fprm = (ffn_w_in[i], ffn_w_dw[i], ffn_b_dw[i], ffn_w_out[i])
        fp, fb = conv_ffn(xp, jnp.zeros((Bp, FFN_CONV_WIDTH - 1, D_FF), xp.dtype), *fprm)
        p_ffn.append(fb)
        fs, fb = conv_ffn(xs, state_ffn_conv[i], *fprm)
        s_ffn.append(fb)
        xp = deepnorm(xp, fp, ln_g[i, 1], ln_b[i, 1])
        xs = deepnorm(xs, fs, ln_g[i, 1], ln_b[i, 1])
    return (xp, xs,
            jnp.stack(p_wkv), jnp.stack(p_shift), jnp.stack(p_conv), jnp.stack(p_ckv), jnp.stack(p_kpe), jnp.stack(p_ffn),
            jnp.stack(s_wkv), jnp.stack(s_shift), jnp.stack(s_conv), jnp.stack(s_ckv), jnp.stack(s_kpe), jnp.stack(s_ffn))
```

```python
import functools

import jax
import jax.numpy as jnp
from jax import lax
from jax.experimental import pallas as pl
from jax.experimental.pallas import tpu as pltpu

D_MODEL = 1024
DEPTH = 4
N_MIXERS = 3
DN_ALPHA = (2 * DEPTH) ** 0.25
LN_EPS = 1e-5
RWKV_HEAD = 64
RWKV_HEADS = D_MODEL // RWKV_HEAD
GN_EPS = 64e-5
CONV_WIDTH = 31
MLA_HEADS = 8
QK_NOPE = 128
QK_ROPE = 64
V_HEAD = 128
KV_LORA = 256
Q_LORA = 384
ROPE_THETA = 10000.0
MLA_SCALE = (QK_NOPE + QK_ROPE) ** -0.5
RMS_EPS = 1e-6
D_FF = 2816
FFN_CONV_WIDTH = 3

SUBLANES = 8
LANES = 128
VMEM_LIMIT = 56 * 1024 * 1024

BF16 = jnp.bfloat16
F32 = jnp.float32


def _params(sem):
    return pltpu.CompilerParams(dimension_semantics=sem, vmem_limit_bytes=VMEM_LIMIT)


def _dot(a, b):
    return jnp.dot(a.astype(BF16), b.astype(BF16), preferred_element_type=F32)


def _dot_t(a, b):
    return lax.dot_general(a.astype(BF16), b.astype(BF16), (((1,), (1,)), ((), ())),
                           preferred_element_type=F32)


def _layer_norm(z, g, b, eps=LN_EPS):
    mu = jnp.mean(z, axis=-1, keepdims=True)
    zc = z - mu
    var = jnp.mean(zc * zc, axis=-1, keepdims=True)
    return zc * lax.rsqrt(var + eps) * g + b


def _sigmoid(z):
    return 1.0 / (1.0 + jnp.exp(-z))


def _silu(z):
    return z * _sigmoid(z)


def _prev_rows(cur, shift, pos, fill):
    out = pltpu.roll(cur, shift, 0)
    for p in range(shift):
        out = jnp.where(pos == p, fill[shift - 1 - p], out)
    return out


def _ffn_body(seq_len, tm, x_ref, halo_ref, wa_ref, wb_ref, wdw_ref, bdw_ref, wo_ref, g_ref, b_ref,
              y_ref, tail_ref, acc_ref):
    i, f = pl.program_id(0), pl.program_id(1)
    x = x_ref[...]
    a = _dot(x, wa_ref[...])
    b = _dot(x, wb_ref[...])
    fc = a.shape[-1]
    rows = lax.broadcasted_iota(jnp.int32, (tm, 1), 0)
    if seq_len >= tm:
        starts_sequence = (i * tm) % seq_len == 0
        ah = _dot(halo_ref[...], wa_ref[...])
        ah = jnp.where(starts_sequence, 0.0, ah)
        pos = rows
        before1, before2 = ah[SUBLANES - 1:SUBLANES], ah[SUBLANES - 2:SUBLANES - 1]
    else:
        hist = halo_ref[...]
        nseq = tm // seq_len
        expand = lambda r: jnp.broadcast_to(r, (nseq, seq_len, fc)).reshape(tm, fc)
        pos = rows % seq_len
        before1, before2 = expand(hist[:, 1:2, :]), expand(hist[:, 0:1, :])
    a1 = _prev_rows(a, 1, pos, [before1])
    a2 = _prev_rows(a, 2, pos, [before1, before2])
    wdw = wdw_ref[...]
    c = a2 * wdw[0:1] + a1 * wdw[1:2] + a * wdw[2:3] + bdw_ref[...]
    part = _dot(_silu(c) * b, wo_ref[...])

    @pl.when(f == 0)
    def _():
        acc_ref[...] = part

    @pl.when(f > 0)
    def _():
        acc_ref[...] += part

    if seq_len >= tm:
        tail_ref[0] = a[tm - SUBLANES:tm]
    else:
        tail_ref[...] = a.reshape(tm // seq_len, seq_len, fc)

    @pl.when(f == pl.num_programs(1) - 1)
    def _():
        y_ref[...] = _layer_norm(DN_ALPHA * x + acc_ref[...], g_ref[...], b_ref[...])


def _conv_ffn(x, hist, w_in, w_dw, b_dw, w_out, ln_g, ln_b, *, tm, fc):
    B, T, D = x.shape
    F = w_out.shape[0]
    M = B * T
    x2 = x.reshape(M, D)
    nF = F // fc
    long_seq = hist is None
    if long_seq:
        tm = min(tm, T)
        assert T % tm == 0 and tm % SUBLANES == 0
        halo = x2
        halo_spec = pl.BlockSpec((SUBLANES, D), lambda i, f: (jnp.maximum(i * (tm // SUBLANES) - 1, 0), 0))
        tail_shape = jax.ShapeDtypeStruct((M // tm, SUBLANES, F), F32)
        tail_spec = pl.BlockSpec((1, SUBLANES, fc), lambda i, f: (i, 0, f))
    else:
        assert T == SUBLANES
        tm = min(tm, M)
        assert M % tm == 0 and tm % T == 0
        halo = hist
        halo_spec = pl.BlockSpec((tm // T, FFN_CONV_WIDTH - 1, fc), lambda i, f: (i, 0, f))
        tail_shape = jax.ShapeDtypeStruct((B, T, F), F32)
        tail_spec = pl.BlockSpec((tm // T, T, fc), lambda i, f: (i, 0, f))
    y, tail = pl.pallas_call(
        functools.partial(_ffn_body, T, tm),
        grid=(M // tm, nF),
        in_specs=[
            pl.BlockSpec((tm, D), lambda i, f: (i, 0)),
            halo_spec,
            pl.BlockSpec((D, fc), lambda i, f: (0, f)),
            pl.BlockSpec((D, fc), lambda i, f: (0, nF + f)),
            pl.BlockSpec((FFN_CONV_WIDTH, fc), lambda i, f: (0, f)),
            pl.BlockSpec((1, fc), lambda i, f: (0, f)),
            pl.BlockSpec((fc, D), lambda i, f: (f, 0)),
            pl.BlockSpec((1, D), lambda i, f: (0, 0)),
            pl.BlockSpec((1, D), lambda i, f: (0, 0)),
        ],
        out_specs=[pl.BlockSpec((tm, D), lambda i, f: (i, 0)), tail_spec],
        out_shape=[jax.ShapeDtypeStruct((M, D), F32), tail_shape],
        scratch_shapes=[pltpu.VMEM((tm, D), F32)],
        compiler_params=_params(("arbitrary", "arbitrary")),
        name="conv_ffn",
    )(x2, halo, w_in, w_in, w_dw, b_dw.reshape(1, F), w_out, ln_g.reshape(1, D), ln_b.reshape(1, D))
    if long_seq:
        tail = tail[T // tm - 1::T // tm]
    return y.reshape(B, T, D), tail[:, -(FFN_CONV_WIDTH - 1):, :]


def _proj_norm_body(gated, *refs):
    if gated:
        x_ref, h_ref, gate_ref, w_ref, bias_ref, g_ref, b_ref, y_ref = refs
        h = h_ref[...] * gate_ref[...]
    else:
        x_ref, h_ref, w_ref, bias_ref, g_ref, b_ref, y_ref = refs
        h = h_ref[...]
    out = _dot(h, w_ref[...]) + bias_ref[...]
    y_ref[...] = _layer_norm(DN_ALPHA * x_ref[...] + out, g_ref[...], b_ref[...])


def _proj_norm(x, h, gate, w, bias, ln_g, ln_b, *, tm):
    M, D = x.shape
    K = h.shape[1]
    tm = min(tm, M)
    assert M % tm == 0
    row = lambda c: pl.BlockSpec((tm, c), lambda i: (i, 0))
    const = lambda r, c: pl.BlockSpec((r, c), lambda i: (0, 0))
    operands = [x, h] + ([gate] if gate is not None else []) + [w, bias.reshape(1, D), ln_g.reshape(1, D), ln_b.reshape(1, D)]
    specs = [row(D), row(K)] + ([row(K)] if gate is not None else []) + [const(K, D), const(1, D), const(1, D), const(1, D)]
    return pl.pallas_call(
        functools.partial(_proj_norm_body, gate is not None),
        grid=(M // tm,),
        in_specs=specs,
        out_specs=row(D),
        out_shape=jax.ShapeDtypeStruct((M, D), F32),
        compiler_params=_params(("arbitrary",)),
        name="proj_norm",
    )(*operands)


def _softplus(z):
    return jnp.maximum(z, 0.0) + jnp.log(1.0 + jnp.exp(-jnp.abs(z)))


def _rwkv_proj_body(seq_len, tm, x_ref, halo_ref, mu_ref, wrkv_ref, w1_ref, w2_ref, a1_ref, a2_ref,
                    g1_ref, g2_ref, vec_ref, r_ref, k_ref, v_ref, d_ref, a_ref, g_ref):
    i = pl.program_id(0)
    x = x_ref[...]
    D = x.shape[-1]
    rows = lax.broadcasted_iota(jnp.int32, (tm, 1), 0)
    if seq_len >= tm:
        starts_sequence = (i * tm) % seq_len == 0
        before = jnp.where(starts_sequence, 0.0, halo_ref[SUBLANES - 1:SUBLANES, :])
        pos = rows
    else:
        nseq = tm // seq_len
        before = jnp.broadcast_to(halo_ref[...], (nseq, seq_len, D)).reshape(tm, D)
        pos = rows % seq_len
    dx = _prev_rows(x, 1, pos, [before]) - x
    mu = mu_ref[...]
    mix = lambda j: x + dx * mu[j:j + 1]
    vec = vec_ref[...]
    r_ref[...] = _dot(mix(0), wrkv_ref[0])
    k_ref[...] = _dot(mix(1), wrkv_ref[1])
    v_ref[...] = _dot(mix(2), wrkv_ref[2])
    w_pre = vec[0:1] + _dot(jnp.tanh(_dot(mix(3), w1_ref[...])), w2_ref[...])
    d_ref[...] = jnp.exp(-jnp.exp(-_softplus(-w_pre) - 0.5))
    a_ref[...] = _sigmoid(vec[1:2] + _dot(_dot(mix(4), a1_ref[...]), a2_ref[...]))
    g_ref[...] = _dot(_sigmoid(_dot(mix(5), g1_ref[...])), g2_ref[...])


def _rwkv_proj(x, shift, mu, w_rkv, w1, w2, a1, a2, g1, g2, vec, *, tm):
    B, T, D = x.shape
    M = B * T
    x2 = x.reshape(M, D)
    if shift is None:
        tm = min(tm, T)
        assert T % tm == 0
        halo = x2
        halo_spec = pl.BlockSpec((SUBLANES, D), lambda i: (jnp.maximum(i * (tm // SUBLANES) - 1, 0), 0))
    else:
        assert T == SUBLANES
        tm = min(tm, M)
        assert M % tm == 0
        halo = shift.reshape(B, 1, D)
        halo_spec = pl.BlockSpec((tm // T, 1, D), lambda i: (i, 0, 0))
    full = lambda arr: pl.BlockSpec(arr.shape, lambda i: (0,) * arr.ndim)
    row = pl.BlockSpec((tm, D), lambda i: (i, 0))
    outs = pl.pallas_call(
        functools.partial(_rwkv_proj_body, T, tm),
        grid=(M // tm,),
        in_specs=[row, halo_spec] + [full(w) for w in (mu, w_rkv, w1, w2, a1, a2, g1, g2, vec)],
        out_specs=[row] * 6,
        out_shape=[jax.ShapeDtypeStruct((M, D), F32)] * 6,
        compiler_params=_params(("arbitrary",)),
        name="rwkv_proj",
    )(x2, halo, mu, w_rkv, w1, w2, a1, a2, g1, g2, vec)
    return [o.reshape(B, T, D) for o in outs]


def _seg_sum(z, low_half):
    s_lo = jnp.sum(jnp.where(low_half, z, 0.0), axis=-1, keepdims=True)
    s_hi = jnp.sum(jnp.where(low_half, 0.0, z), axis=-1, keepdims=True)
    return jnp.where(low_half, s_lo, s_hi)


def _wkv_body(Gb, Gp, Tc, U, has_state, r_ref, k_ref, v_ref, d_ref, a_ref, vec_ref, *rest):
    if has_state:
        s0_ref, y_ref, sT_ref, S_scr = rest
    else:
        y_ref, sT_ref, S_scr = rest
    N = RWKV_HEAD
    ic = pl.program_id(2)
    chains = [(gb, gp) for gb in range(Gb) for gp in range(Gp)]
    low64 = lax.broadcasted_iota(jnp.int32, (N, LANES), 1) < N
    lowU = lax.broadcasted_iota(jnp.int32, (U, LANES), 1) < N
    lane_mod = lax.broadcasted_iota(jnp.int32, (N, LANES), 1) % N

    @pl.when(ic == 0)
    def _():
        for c, (gb, gp) in enumerate(chains):
            if has_state:
                S_scr[c] = jnp.concatenate([s0_ref[gb, 2 * gp], s0_ref[gb, 2 * gp + 1]], axis=-1)
            else:
                S_scr[c] = jnp.zeros((N, LANES), F32)

    def sub_chunk(j, carry):
        off = pl.multiple_of(j * U, U)
        pre = []
        for c, (gb, gp) in enumerate(chains):
            lanes = slice(LANES * gp, LANES * (gp + 1))
            sl = (gb, pl.ds(off, U), lanes)
            r, k, v, d, a = r_ref[sl], k_ref[sl], v_ref[sl], d_ref[sl], a_ref[sl]
            vec = vec_ref[:, lanes]
            k_k, k_a, r_k = vec[0:1], vec[1:2], vec[4:5]
            kk = k * k_k
            kk = kk / jnp.maximum(jnp.sqrt(_seg_sum(kk * kk, lowU)), 1e-12)
            kh = k * (1.0 + (a - 1.0) * k_a)
            bonus = _seg_sum(r * kh * r_k, lowU) * v
            pre.append(dict(r=r, kh=kh, d=d, av=-kk, bv=kk * a, vT=v.T, bonus=bonus, S=S_scr[c],
                            yacc=jnp.zeros((N, LANES), F32)))
        for t in range(U):
            for p in pre:
                row = lambda z: z[t:t + 1, :]
                S = p["S"]
                sa = _seg_sum(S * row(p["av"]), low64)
                vcol = jnp.where(low64, p["vT"][0:N, t:t + 1], p["vT"][N:2 * N, t:t + 1])
                S = S * row(p["d"]) + sa * row(p["bv"]) + vcol * row(p["kh"])
                ycol = _seg_sum(S * row(p["r"]), low64)
                p["yacc"] = jnp.where(lane_mod == t, ycol, p["yacc"])
                p["S"] = S
        for c, (gb, gp) in enumerate(chains):
            p = pre[c]
            lanes = slice(LANES * gp, LANES * (gp + 1))
            S_scr[c] = p["S"]
            yT = p["yacc"].T
            y = jnp.concatenate([yT[0:U, :], yT[N:N + U, :]], axis=-1)
            vec = vec_ref[:, lanes]
            mean = _seg_sum(y, lowU) * (1.0 / N)
            yc = y - mean
            var = _seg_sum(yc * yc, lowU) * (1.0 / N)
            y_ref[gb, pl.ds(off, U), lanes] = yc * lax.rsqrt(var + GN_EPS) * vec[2:3] + vec[3:4] + p["bonus"]
        return carry

    lax.fori_loop(0, Tc // U, sub_chunk, 0)

    @pl.when(ic == pl.num_programs(2) - 1)
    def _():
        for c, (gb, gp) in enumerate(chains):
            S = S_scr[c]
            sT_ref[gb, 2 * gp] = S[:, 0:N]
            sT_ref[gb, 2 * gp + 1] = S[:, N:2 * N]


def _wkv(r, k, v, d, a, vec8, state, *, Gb, Gp, Tc, U):
    B, T, D = r.shape
    H, N = RWKV_HEADS, RWKV_HEAD
    Tc = min(Tc, T)
    U = min(U, Tc)
    assert U <= N and T % Tc == 0 and Tc % U == 0 and B % Gb == 0 and (H // 2) % Gp == 0
    seq = pl.BlockSpec((Gb, Tc, LANES * Gp), lambda b, p, c: (b, c, p))
    st = pl.BlockSpec((Gb, 2 * Gp, N, N), lambda b, p, c: (b, p, 0, 0))
    operands = [r, k, v, d, a, vec8] + ([state] if state is not None else [])
    specs = [seq] * 5 + [pl.BlockSpec((8, LANES * Gp), lambda b, p, c: (0, p))] + ([st] if state is not None else [])
    return pl.pallas_call(
        functools.partial(_wkv_body, Gb, Gp, Tc, U, state is not None),
        grid=(B // Gb, (H // 2) // Gp, T // Tc),
        in_specs=specs,
        out_specs=[seq, st],
        out_shape=[jax.ShapeDtypeStruct((B, T, D), F32), jax.ShapeDtypeStruct((B, H, N, N), F32)],
        scratch_shapes=[pltpu.VMEM((Gb * Gp, N, LANES), F32)],
        compiler_params=_params(("arbitrary", "arbitrary", "arbitrary")),
        name="wkv",
    )(*operands)


def _rwkv_layer(x, shift, wkv_state, prm, ln_g, ln_b, *, tm, wkv_cfg):
    mu, w_rkv, w1, w2, a1, a2, g1, g2, vec, r_k, w_o = prm
    B, T, D = x.shape
    r, k, v, d, a, g = _rwkv_proj(x, shift, mu, w_rkv, w1, w2, a1, a2, g1, g2, vec, tm=tm)
    vec8 = jnp.concatenate([vec[2:6], r_k.reshape(1, D), jnp.zeros((3, D), F32)], axis=0)
    y, s_new = _wkv(r, k, v, d, a, vec8, wkv_state, **wkv_cfg)
    x_new = _proj_norm(x.reshape(B * T, D), y.reshape(B * T, D), g.reshape(B * T, D), w_o,
                       jnp.zeros((D,), F32), ln_g, ln_b, tm=tm)
    return x_new.reshape(B, T, D), s_new, x[:, -1]


CONV_HIST = CONV_WIDTH - 1
CONV_HALO = 32


def _glu_body(x_ref, w_ref, b_ref, u_ref):
    h = _dot(x_ref[...], w_ref[...]) + b_ref[...]
    D = u_ref.shape[-1]
    u_ref[...] = h[:, :D] * _sigmoid(h[:, D:])


def _glu(x2, w, b, *, tm):
    M, D = x2.shape
    tm = min(tm, M)
    assert M % tm == 0
    return pl.pallas_call(
        _glu_body,
        grid=(M // tm,),
        in_specs=[pl.BlockSpec((tm, D), lambda i: (i, 0)),
                  pl.BlockSpec((D, 2 * D), lambda i: (0, 0)),
                  pl.BlockSpec((1, 2 * D), lambda i: (0, 0))],
        out_specs=pl.BlockSpec((tm, D), lambda i: (i, 0)),
        out_shape=jax.ShapeDtypeStruct((M, D), F32),
        compiler_params=_params(("arbitrary",)),
        name="conf_glu",
    )(x2, w, b.reshape(1, 2 * D))


def _conf_tail(conv, x, bdw_ref, cln_ref, w2_ref, b2_ref, g_ref, b_ref):
    cln = cln_ref[...]
    c = _silu(_layer_norm(conv + bdw_ref[...], cln[0:1], cln[1:2]))
    out = _dot(c, w2_ref[...]) + b2_ref[...]
    return _layer_norm(DN_ALPHA * x + out, g_ref[...], b_ref[...])


def _conf_long_body(seq_len, tm, x_ref, u_ref, halo_ref, wdw_ref, bdw_ref, cln_ref, w2_ref, b2_ref, g_ref, b_ref,
                    y_ref, ext_ref):
    i = pl.program_id(0)
    starts_sequence = (i * tm) % seq_len == 0
    ext_ref[0:CONV_HALO, :] = jnp.where(starts_sequence, 0.0, halo_ref[...])
    ext_ref[CONV_HALO:CONV_HALO + tm, :] = u_ref[...]
    wdw = wdw_ref[...]
    first = CONV_HALO - CONV_HIST
    conv = ext_ref[first:first + tm, :] * wdw[0:1]
    for j in range(1, CONV_WIDTH):
        conv = conv + ext_ref[first + j:first + j + tm, :] * wdw[j:j + 1]
    y_ref[...] = _conf_tail(conv, x_ref[...], bdw_ref, cln_ref, w2_ref, b2_ref, g_ref, b_ref)


def _conf_short_body(seq_len, nb, x_ref, u_ref, hist_ref, wdw_ref, bdw_ref, cln_ref, w2_ref, b2_ref, g_ref, b_ref,
                     y_ref, hist_out_ref, ext_ref):
    D = u_ref.shape[-1]
    ext_ref[:, 0:CONV_HIST, :] = hist_ref[...]
    ext_ref[:, CONV_HIST:CONV_HIST + seq_len, :] = u_ref[...].reshape(nb, seq_len, D)
    wdw = wdw_ref[...]
    conv = ext_ref[:, 0:seq_len, :] * wdw[0:1]
    for j in range(1, CONV_WIDTH):
        conv = conv + ext_ref[:, j:j + seq_len, :] * wdw[j:j + 1]
    hist_out_ref[...] = ext_ref[:, seq_len:seq_len + CONV_HIST, :]
    y_ref[...] = _conf_tail(conv.reshape(nb * seq_len, D), x_ref[...], bdw_ref, cln_ref, w2_ref, b2_ref, g_ref, b_ref)


def _conformer_layer(x, hist, prm, ln_g, ln_b, *, tm):
    w_pw1, b_pw1, w_dw, b_dw, cln, w_pw2, b_pw2 = prm
    B, T, D = x.shape
    M = B * T
    x2 = x.reshape(M, D)
    u = _glu(x2, w_pw1, b_pw1, tm=tm)
    const = lambda r, c: pl.BlockSpec((r, c), lambda i: (0, 0))
    weights = [w_dw, b_dw.reshape(1, D), cln, w_pw2, b_pw2.reshape(1, D), ln_g.reshape(1, D), ln_b.reshape(1, D)]
    wspecs = [const(CONV_WIDTH, D), const(1, D), const(2, D), const(D, D), const(1, D), const(1, D), const(1, D)]
    if hist is None:
        tm = min(tm, T)
        assert T % tm == 0 and tm % CONV_HALO == 0
        row = pl.BlockSpec((tm, D), lambda i: (i, 0))
        halo_spec = pl.BlockSpec((CONV_HALO, D), lambda i: (jnp.maximum(i * (tm // CONV_HALO) - 1, 0), 0))
        y = pl.pallas_call(
            functools.partial(_conf_long_body, T, tm),
            grid=(M // tm,),
            in_specs=[row, row, halo_spec] + wspecs,
            out_specs=row,
            out_shape=jax.ShapeDtypeStruct((M, D), F32),
            scratch_shapes=[pltpu.VMEM((CONV_HALO + tm, D), F32)],
            compiler_params=_params(("arbitrary",)),
            name="conf_conv_long",
        )(x2, u, u, *weights)
        new_hist = u.reshape(B, T, D)[:, T - CONV_HIST:, :]
    else:
        assert T == SUBLANES
        nb = min(tm // T, B)
        assert B % nb == 0
        row = pl.BlockSpec((nb * T, D), lambda i: (i, 0))
        hspec = pl.BlockSpec((nb, CONV_HIST, D), lambda i: (i, 0, 0))
        y, new_hist = pl.pallas_call(
            functools.partial(_conf_short_body, T, nb),
            grid=(B // nb,),
            in_specs=[row, row, hspec] + wspecs,
            out_specs=[row, hspec],
            out_shape=[jax.ShapeDtypeStruct((M, D), F32), jax.ShapeDtypeStruct((B, CONV_HIST, D), F32)],
            scratch_shapes=[pltpu.VMEM((nb, CONV_HIST + T + 2, D), F32)],
            compiler_params=_params(("arbitrary",)),
            name="conf_conv_short",
        )(x2, u, hist, *weights)
    return y.reshape(B, T, D), new_hist


def _rms_norm(z, g):
    return z * lax.rsqrt(jnp.mean(z * z, axis=-1, keepdims=True) + RMS_EPS) * g


def _mla_proj_body(x_ref, cos_ref, sin_ref, wdq_ref, qn_ref, wqn_ref, wqr_ref, wqs_ref, wckv_ref, wkr_ref, wks_ref,
                   kvn_ref, wuk_ref, qlat_ref, qpe_ref, ckv_ref, kpe_ref):
    x = x_ref[...]
    cos, sin = cos_ref[...], sin_ref[...]
    cq = _rms_norm(_dot(x, wdq_ref[...]), qn_ref[...])
    q_nope = _dot(cq, wqn_ref[...])
    q_rope = _dot(cq, wqr_ref[...])
    q_swap = _dot(cq, wqs_ref[...])
    for h in range(MLA_HEADS):
        qlat_ref[h] = _dot(q_nope[:, h * QK_NOPE:(h + 1) * QK_NOPE], wuk_ref[h])
        sl = slice(h * QK_ROPE, (h + 1) * QK_ROPE)
        qpe_ref[h] = q_rope[:, sl] * cos + q_swap[:, sl] * sin
    ckv_ref[...] = _rms_norm(_dot(x, wckv_ref[...]), kvn_ref[...])
    kpe_ref[...] = _dot(x, wkr_ref[...]) * cos + _dot(x, wks_ref[...]) * sin


def _swap_halves(w):
    half = w.shape[-1] // 2
    return jnp.concatenate([w[..., half:], w[..., :half]], axis=-1)


def _rope_tables(pos):
    half = QK_ROPE // 2
    inv = ROPE_THETA ** (-jnp.arange(half, dtype=F32) / half)
    ang = pos[:, None] * inv[None, :]
    cos, sin = jnp.cos(ang), jnp.sin(ang)
    return jnp.concatenate([cos, cos], -1), jnp.concatenate([-sin, sin], -1)


def _mla_proj(x2, pos, w_dq, q_norm, w_uq, w_dkv, kv_norm, w_uk, *, tm):
    M, D = x2.shape
    H = MLA_HEADS
    tm = min(tm, M)
    assert M % tm == 0
    cos, sin = _rope_tables(pos)
    w_qn = w_uq[:, :, :QK_NOPE].reshape(Q_LORA, H * QK_NOPE)
    w_qr = w_uq[:, :, QK_NOPE:].reshape(Q_LORA, H * QK_ROPE)
    w_qs = _swap_halves(w_uq[:, :, QK_NOPE:]).reshape(Q_LORA, H * QK_ROPE)
    w_ckv, w_kr = w_dkv[:, :KV_LORA], w_dkv[:, KV_LORA:]
    w_ks = _swap_halves(w_kr)
    w_ukT = jnp.transpose(w_uk, (1, 2, 0))
    full = lambda arr: pl.BlockSpec(arr.shape, lambda i: (0,) * arr.ndim)
    row = lambda c: pl.BlockSpec((tm, c), lambda i: (i, 0))
    hrow = lambda c: pl.BlockSpec((H, tm, c), lambda i: (0, i, 0))
    weights = [w_dq, q_norm.reshape(1, Q_LORA), w_qn, w_qr, w_qs, w_ckv, w_kr, w_ks, kv_norm.reshape(1, KV_LORA), w_ukT]
    return pl.pallas_call(
        _mla_proj_body,
        grid=(M // tm,),
        in_specs=[row(D), row(QK_ROPE), row(QK_ROPE)] + [full(w) for w in weights],
        out_specs=[hrow(KV_LORA), hrow(QK_ROPE), row(KV_LORA), row(QK_ROPE)],
        out_shape=[jax.ShapeDtypeStruct((H, M, KV_LORA), F32), jax.ShapeDtypeStruct((H, M, QK_ROPE), F32),
                   jax.ShapeDtypeStruct((M, KV_LORA), F32), jax.ShapeDtypeStruct((M, QK_ROPE), F32)],
        compiler_params=_params(("arbitrary",)),
        name="mla_proj",
    )(x2, cos, sin, *weights)


def _scores(ql, qp, ckv, kpe):
    return (_dot_t(ql, ckv) + _dot_t(qp, kpe)) * MLA_SCALE


def _softmax_step(s, ckv, m_ref, l_ref, acc_ref):
    m_old = m_ref[...]
    m_new = jnp.maximum(m_old, jnp.max(s, axis=-1, keepdims=True))
    p = jnp.exp(s - m_new)
    alpha = jnp.exp(m_old - m_new)
    l_ref[...] = alpha * l_ref[...] + jnp.sum(p, axis=-1, keepdims=True)
    acc_ref[...] = alpha * acc_ref[...] + _dot(p, ckv)
    m_ref[...] = m_new


def _flash_body(tq, tk, qlat_ref, qpe_ref, ckv_ref, kpe_ref, o_ref, m_ref, l_ref, acc_ref):
    i, j = pl.program_id(1), pl.program_id(2)
    H = MLA_HEADS
    last_j = ((i + 1) * tq - 1) // tk

    @pl.when(j == 0)
    def _():
        m_ref[...] = jnp.full(m_ref.shape, -jnp.inf, F32)
        l_ref[...] = jnp.zeros(l_ref.shape, F32)
        acc_ref[...] = jnp.zeros(acc_ref.shape, F32)

    @pl.when(j <= last_j)
    def _():
        ql = qlat_ref[...].reshape(H * tq, KV_LORA)
        qp = qpe_ref[...].reshape(H * tq, QK_ROPE)
        ckv = ckv_ref[...]
        s = _scores(ql, qp, ckv, kpe_ref[...])
        q_pos = i * tq + lax.broadcasted_iota(jnp.int32, (H * tq, 1), 0) % tq
        k_pos = j * tk + lax.broadcasted_iota(jnp.int32, (1, tk), 1)
        s = jnp.where(k_pos <= q_pos, s, -jnp.inf)
        _softmax_step(s, ckv, m_ref, l_ref, acc_ref)

    @pl.when(j == last_j)
    def _():
        o_ref[...] = (acc_ref[...] / l_ref[...]).reshape(H, tq, KV_LORA)


def _flash_attention(q_lat, q_pe, ckv, kpe, B, T, *, tq, tk):
    H = MLA_HEADS
    tq, tk = min(tq, T), min(tk, T)
    assert T % tq == 0 and T % tk == 0
    nq, nk = T // tq, T // tk
    kmap = lambda b, i, j: (b * nk + jnp.minimum(j, ((i + 1) * tq - 1) // tk), 0)
    qmap = lambda b, i, j: (0, b * nq + i, 0)
    return pl.pallas_call(
        functools.partial(_flash_body, tq, tk),
        grid=(B, nq, nk),
        in_specs=[pl.BlockSpec((H, tq, KV_LORA), qmap), pl.BlockSpec((H, tq, QK_ROPE), qmap),
                  pl.BlockSpec((tk, KV_LORA), kmap), pl.BlockSpec((tk, QK_ROPE), kmap)],
        out_specs=pl.BlockSpec((H, tq, KV_LORA), qmap),
        out_shape=jax.ShapeDtypeStruct((H, B * T, KV_LORA), F32),
        scratch_shapes=[pltpu.VMEM((H * tq, 1), F32), pltpu.VMEM((H * tq, 1), F32), pltpu.VMEM((H * tq, KV_LORA), F32)],
        compiler_params=_params(("arbitrary", "arbitrary", "arbitrary")),
        name="mla_flash",
    )(q_lat, q_pe, ckv, kpe)


def _paged_body(PG, T, page, pt_ref, qlat_ref, qpe_ref, ckvn_ref, kpen_ref, *rest):
    pages_ckv, pages_kpe = rest[:PG], rest[PG:2 * PG]
    o_ref, m_ref, l_ref, acc_ref, selfk_ref, selfp_ref = rest[2 * PG:]
    g = pl.program_id(1)
    H = MLA_HEADS
    ql = qlat_ref[...].reshape(H * T, KV_LORA)
    qp = qpe_ref[...].reshape(H * T, QK_ROPE)

    @pl.when(g == 0)
    def _():
        m_ref[...] = jnp.full(m_ref.shape, -jnp.inf, F32)
        l_ref[...] = jnp.zeros(l_ref.shape, F32)
        acc_ref[...] = jnp.zeros(acc_ref.shape, F32)

    for p in range(PG):
        ckv = pages_ckv[p][0]
        _softmax_step(_scores(ql, qp, ckv, pages_kpe[p][0]), ckv, m_ref, l_ref, acc_ref)

    @pl.when(g == pl.num_programs(1) - 1)
    def _():
        selfk_ref[...] = jnp.zeros(selfk_ref.shape, F32)
        selfp_ref[...] = jnp.zeros(selfp_ref.shape, F32)
        selfk_ref[0:T, :] = ckvn_ref[...]
        selfp_ref[0:T, :] = kpen_ref[...]
        ckv = selfk_ref[...]
        s = _scores(ql, qp, ckv, selfp_ref[...])
        q_pos = lax.broadcasted_iota(jnp.int32, (H * T, 1), 0) % T
        k_pos = lax.broadcasted_iota(jnp.int32, (1, page), 1)
        s = jnp.where(k_pos <= q_pos, s, -jnp.inf)
        _softmax_step(s, ckv, m_ref, l_ref, acc_ref)
        o_ref[...] = (acc_ref[...] / l_ref[...]).reshape(H, T, KV_LORA)


def _paged_attention(q_lat, q_pe, ckv, kpe, pool_ckv, pool_kpe, page_table, T, *, PG):
    H = MLA_HEADS
    DB, n_pages = page_table.shape
    page = pool_ckv.shape[1]
    assert n_pages % PG == 0 and T <= page
    qmap = lambda b, g, pt: (0, b, 0)
    nmap = lambda b, g, pt: (b, 0)
    pmap = lambda p: (lambda b, g, pt: (pt[b * n_pages + g * PG + p], 0, 0))
    grid_spec = pltpu.PrefetchScalarGridSpec(
        num_scalar_prefetch=1,
        grid=(DB, n_pages // PG),
        in_specs=[pl.BlockSpec((H, T, KV_LORA), qmap), pl.BlockSpec((H, T, QK_ROPE), qmap),
                  pl.BlockSpec((T, KV_LORA), nmap), pl.BlockSpec((T, QK_ROPE), nmap)]
                 + [pl.BlockSpec((1, page, KV_LORA), pmap(p)) for p in range(PG)]
                 + [pl.BlockSpec((1, page, QK_ROPE), pmap(p)) for p in range(PG)],
        out_specs=pl.BlockSpec((H, T, KV_LORA), qmap),
        scratch_shapes=[pltpu.VMEM((H * T, 1), F32), pltpu.VMEM((H * T, 1), F32), pltpu.VMEM((H * T, KV_LORA), F32),
                        pltpu.VMEM((page, KV_LORA), F32), pltpu.VMEM((page, QK_ROPE), F32)],
    )
    return pl.pallas_call(
        functools.partial(_paged_body, PG, T, page),
        grid_spec=grid_spec,
        out_shape=jax.ShapeDtypeStruct((H, DB * T, KV_LORA), F32),
        compiler_params=_params(("arbitrary", "arbitrary")),
        name="mla_paged",
    )(page_table.reshape(-1), q_lat, q_pe, ckv, kpe, *([pool_ckv] * PG), *([pool_kpe] * PG))


def _mla_out_body(x_ref, o_ref, wuv_ref, wo_ref, g_ref, b_ref, y_ref):
    heads = [_dot(o_ref[h], wuv_ref[h]) for h in range(MLA_HEADS)]
    out = _dot(jnp.concatenate(heads, axis=-1), wo_ref[...])
    y_ref[...] = _layer_norm(DN_ALPHA * x_ref[...] + out, g_ref[...], b_ref[...])


def _mla_out(x2, o_lat, w_uv, w_o, ln_g, ln_b, *, tm):
    M, D = x2.shape
    H = MLA_HEADS
    tm = min(tm, M)
    assert M % tm == 0
    w_uvh = jnp.transpose(w_uv, (1, 0, 2))
    const = lambda shape: pl.BlockSpec(shape, lambda i: (0,) * len(shape))
    return pl.pallas_call(
        _mla_out_body,
        grid=(M // tm,),
        in_specs=[pl.BlockSpec((tm, D), lambda i: (i, 0)), pl.BlockSpec((H, tm, KV_LORA), lambda i: (0, i, 0)),
                  const(w_uvh.shape), const(w_o.shape), const((1, D)), const((1, D))],
        out_specs=pl.BlockSpec((tm, D), lambda i: (i, 0)),
        out_shape=jax.ShapeDtypeStruct((M, D), F32),
        compiler_params=_params(("arbitrary",)),
        name="mla_out",
    )(x2, o_lat, w_uvh, w_o, ln_g.reshape(1, D), ln_b.reshape(1, D))


def _mla_layer(x, pos0, cache, prm, ln_g, ln_b, *, tm, attn_cfg):
    w_dq, q_norm, w_uq, w_dkv, kv_norm, w_uk, w_uv, w_o = prm
    B, T, D = x.shape
    x2 = x.reshape(B * T, D)
    pos = jnp.tile(pos0 + jnp.arange(T, dtype=F32), B)
    q_lat, q_pe, ckv, kpe = _mla_proj(x2, pos, w_dq, q_norm, w_uq, w_dkv, kv_norm, w_uk, tm=tm)
    if cache is None:
        o_lat = _flash_attention(q_lat, q_pe, ckv, kpe, B, T, **attn_cfg)
    else:
        o_lat = _paged_attention(q_lat, q_pe, ckv, kpe, *cache, T, **attn_cfg)
    y = _mla_out(x2, o_lat, w_uv, w_o, ln_g, ln_b, tm=tm)
    return y.reshape(B, T, D), ckv.reshape(B, T, KV_LORA), kpe.reshape(B, T, QK_ROPE)


def kernel(x_prompt, x_sample, state_wkv, state_shift, state_conv, cache_mla_ckv, cache_mla_kpe, state_ffn_conv, page_table, ln_g, ln_b, rwkv_mu, rwkv_w_rkv, rwkv_w1, rwkv_w2, rwkv_a1, rwkv_a2, rwkv_g1, rwkv_g2, rwkv_vec, rwkv_r_k, rwkv_w_o, conf_w_pw1, conf_b_pw1, conf_w_dw, conf_b_dw, conf_ln, conf_w_pw2, conf_b_pw2, mla_w_dq, mla_q_norm, mla_w_uq, mla_w_dkv, mla_kv_norm, mla_w_uk, mla_w_uv, mla_w_o, ffn_w_in, ffn_w_dw, ffn_b_dw, ffn_w_out):
    bf = lambda w: w.astype(BF16)
    past_len = page_table.shape[1] * cache_mla_ckv.shape[2]
    xp, xs = x_prompt, x_sample
    out_p = dict(wkv=[], shift=[], conv=[], ckv=[], kpe=[], ffn=[])
    out_s = dict(wkv=[], shift=[], conv=[], ckv=[], kpe=[], ffn=[])
    for i in range(DEPTH):
        j, kind = i // N_MIXERS, i % N_MIXERS
        g0, b0, g1, b1 = ln_g[i, 0], ln_b[i, 0], ln_g[i, 1], ln_b[i, 1]
        if kind == 0:
            prm = (rwkv_mu[j], bf(rwkv_w_rkv[j]), bf(rwkv_w1[j]), bf(rwkv_w2[j]), bf(rwkv_a1[j]), bf(rwkv_a2[j]),
                   bf(rwkv_g1[j]), bf(rwkv_g2[j]), rwkv_vec[j], rwkv_r_k[j], bf(rwkv_w_o[j]))
            xp, st, sh = _rwkv_layer(xp, None, None, prm, g0, b0, tm=512, wkv_cfg=dict(Gb=2, Gp=2, Tc=128, U=32))
            out_p["wkv"].append(st)
            out_p["shift"].append(sh)
            xs, st, sh = _rwkv_layer(xs, state_shift[j], state_wkv[j], prm, g0, b0, tm=512,
                                     wkv_cfg=dict(Gb=8, Gp=2, Tc=8, U=8))
            out_s["wkv"].append(st)
            out_s["shift"].append(sh)
        elif kind == 1:
            prm = (bf(conf_w_pw1[j]), conf_b_pw1[j], conf_w_dw[j], conf_b_dw[j], conf_ln[j], bf(conf_w_pw2[j]),
                   conf_b_pw2[j])
            xp, cb = _conformer_layer(xp, None, prm, g0, b0, tm=256)
            out_p["conv"].append(cb)
            xs, cb = _conformer_layer(xs, state_conv[j], prm, g0, b0, tm=256)
            out_s["conv"].append(cb)
        else:
            prm = (bf(mla_w_dq[j]), mla_q_norm[j], bf(mla_w_uq[j]), bf(mla_w_dkv[j]), mla_kv_norm[j],
                   bf(mla_w_uk[j]), bf(mla_w_uv[j]), bf(mla_w_o[j]))
            xp, ckv, kpe = _mla_layer(xp, 0.0, None, prm, g0, b0, tm=512, attn_cfg=dict(tq=128, tk=512))
            out_p["ckv"].append(ckv)
            out_p["kpe"].append(kpe)
            xs, ckv, kpe = _mla_layer(xs, float(past_len), (cache_mla_ckv[j], cache_mla_kpe[j], page_table), prm,
                                      g0, b0, tm=512, attn_cfg=dict(PG=8))
            out_s["ckv"].append(ckv)
            out_s["kpe"].append(kpe)
        w_in, w_out = bf(ffn_w_in[i]), bf(ffn_w_out[i])
        xp, fb = _conv_ffn(xp, None, w_in, ffn_w_dw[i], ffn_b_dw[i], w_out, g1, b1, tm=1024, fc=256)
        out_p["ffn"].append(fb)
        xs, fb = _conv_ffn(xs, state_ffn_conv[i], w_in, ffn_w_dw[i], ffn_b_dw[i], w_out, g1, b1, tm=1024, fc=256)
        out_s["ffn"].append(fb)
    names = ("wkv", "shift", "conv", "ckv", "kpe", "ffn")
    return (xp, xs) + tuple(jnp.stack(out_p[n]) for n in names) + tuple(jnp.stack(out_s[n]) for n in names)
```

```python
import functools

import jax
import jax.numpy as jnp
from jax import lax
from jax.experimental import pallas as pl
from jax.experimental.pallas import tpu as pltpu

D_MODEL = 1024
DEPTH = 4
N_MIXERS = 3
DN_ALPHA = (2 * DEPTH) ** 0.25
LN_EPS = 1e-5
RWKV_HEAD = 64
RWKV_HEADS = D_MODEL // RWKV_HEAD
GN_EPS = 64e-5
CONV_WIDTH = 31
MLA_HEADS = 8
QK_NOPE = 128
QK_ROPE = 64
V_HEAD = 128
KV_LORA = 256
Q_LORA = 384
ROPE_THETA = 10000.0
MLA_SCALE = (QK_NOPE + QK_ROPE) ** -0.5
RMS_EPS = 1e-6
D_FF = 2816
FFN_CONV_WIDTH = 3

SUBLANES = 8
LANES = 128
VMEM_LIMIT = 56 * 1024 * 1024

BF16 = jnp.bfloat16
F32 = jnp.float32


def _params(sem):
    return pltpu.CompilerParams(dimension_semantics=sem, vmem_limit_bytes=VMEM_LIMIT)


def _dot(a, b):
    return jnp.dot(a.astype(BF16), b.astype(BF16), preferred_element_type=F32)


def _dot_t(a, b):
    return lax.dot_general(a.astype(BF16), b.astype(BF16), (((1,), (1,)), ((), ())),
                           preferred_element_type=F32)


def _layer_norm(z, g, b, eps=LN_EPS):
    mu = jnp.mean(z, axis=-1, keepdims=True)
    zc = z - mu
    var = jnp.mean(zc * zc, axis=-1, keepdims=True)
    return zc * lax.rsqrt(var + eps) * g + b


def _sigmoid(z):
    return 1.0 / (1.0 + jnp.exp(-z))


def _silu(z):
    return z * _sigmoid(z)


def _prev_rows(cur, shift, pos, fill):
    out = pltpu.roll(cur, shift, 0)
    for p in range(shift):
        out = jnp.where(pos == p, fill[shift - 1 - p], out)
    return out


def _ffn_body(seq_len, tm, x_ref, halo_ref, wa_ref, wb_ref, wdw_ref, bdw_ref, wo_ref, g_ref, b_ref,
              y_ref, tail_ref, acc_ref, xb_ref):
    i, f = pl.program_id(0), pl.program_id(1)

    @pl.when(f == 0)
    def _():
        xb_ref[...] = x_ref[...].astype(BF16)

    xb = xb_ref[...]
    a = _dot(xb, wa_ref[...])
    b = _dot(xb, wb_ref[...])
    fc = a.shape[-1]
    rows = lax.broadcasted_iota(jnp.int32, (tm, 1), 0)
    if seq_len >= tm:
        starts_sequence = (i * tm) % seq_len == 0
        ah = _dot(halo_ref[...], wa_ref[...])
        ah = jnp.where(starts_sequence, 0.0, ah)
        pos = rows
        before1, before2 = ah[SUBLANES - 1:SUBLANES], ah[SUBLANES - 2:SUBLANES - 1]
    else:
        hist = halo_ref[...]
        nseq = tm // seq_len
        expand = lambda r: jnp.broadcast_to(r, (nseq, seq_len, fc)).reshape(tm, fc)
        pos = rows % seq_len
        before1, before2 = expand(hist[:, 1:2, :]), expand(hist[:, 0:1, :])
    a1 = _prev_rows(a, 1, pos, [before1])
    a2 = _prev_rows(a, 2, pos, [before1, before2])
    wdw = wdw_ref[...]
    c = a2 * wdw[0:1] + a1 * wdw[1:2] + a * wdw[2:3] + bdw_ref[...]
    part = _dot(_silu(c) * b, wo_ref[...])

    @pl.when(f == 0)
    def _():
        acc_ref[...] = part

    @pl.when(f > 0)
    def _():
        acc_ref[...] += part

    if seq_len >= tm:
        tail_ref[0] = a[tm - SUBLANES:tm]
    else:
        tail_ref[...] = a.reshape(tm // seq_len, seq_len, fc)

    @pl.when(f == pl.num_programs(1) - 1)
    def _():
        y_ref[...] = _layer_norm(DN_ALPHA * x_ref[...] + acc_ref[...], g_ref[...], b_ref[...])


def _conv_ffn(x, hist, w_in, w_dw, b_dw, w_out, ln_g, ln_b, *, tm, fc):
    B, T, D = x.shape
    F = w_out.shape[0]
    M = B * T
    x2 = x.reshape(M, D)
    nF = F // fc
    long_seq = hist is None
    if long_seq:
        tm = min(tm, T)
        assert T % tm == 0 and tm % SUBLANES == 0
        halo = x2
        halo_spec = pl.BlockSpec((SUBLANES, D), lambda i, f: (jnp.maximum(i * (tm // SUBLANES) - 1, 0), 0))
        tail_shape = jax.ShapeDtypeStruct((M // tm, SUBLANES, F), F32)
        tail_spec = pl.BlockSpec((1, SUBLANES, fc), lambda i, f: (i, 0, f))
    else:
        assert T == SUBLANES
        tm = min(tm, M)
        assert M % tm == 0 and tm % T == 0
        halo = hist
        halo_spec = pl.BlockSpec((tm // T, FFN_CONV_WIDTH - 1, fc), lambda i, f: (i, 0, f))
        tail_shape = jax.ShapeDtypeStruct((B, T, F), F32)
        tail_spec = pl.BlockSpec((tm // T, T, fc), lambda i, f: (i, 0, f))
    y, tail = pl.pallas_call(
        functools.partial(_ffn_body, T, tm),
        grid=(M // tm, nF),
        in_specs=[
            pl.BlockSpec((tm, D), lambda i, f: (i, 0)),
            halo_spec,
            pl.BlockSpec((D, fc), lambda i, f: (0, f)),
            pl.BlockSpec((D, fc), lambda i, f: (0, nF + f)),
            pl.BlockSpec((FFN_CONV_WIDTH, fc), lambda i, f: (0, f)),
            pl.BlockSpec((1, fc), lambda i, f: (0, f)),
            pl.BlockSpec((fc, D), lambda i, f: (f, 0)),
            pl.BlockSpec((1, D), lambda i, f: (0, 0)),
            pl.BlockSpec((1, D), lambda i, f: (0, 0)),
        ],
        out_specs=[pl.BlockSpec((tm, D), lambda i, f: (i, 0)), tail_spec],
        out_shape=[jax.ShapeDtypeStruct((M, D), F32), tail_shape],
        scratch_shapes=[pltpu.VMEM((tm, D), F32), pltpu.VMEM((tm, D), BF16)],
        compiler_params=_params(("arbitrary", "arbitrary")),
        name="conv_ffn",
    )(x2, halo, w_in, w_in, w_dw, b_dw.reshape(1, F), w_out, ln_g.reshape(1, D), ln_b.reshape(1, D))
    if long_seq:
        tail = tail[T // tm - 1::T // tm]
    return y.reshape(B, T, D), tail[:, -(FFN_CONV_WIDTH - 1):, :]


def _proj_norm_body(gated, *refs):
    if gated:
        x_ref, h_ref, gate_ref, w_ref, bias_ref, g_ref, b_ref, y_ref = refs
        h = h_ref[...] * gate_ref[...]
    else:
        x_ref, h_ref, w_ref, bias_ref, g_ref, b_ref, y_ref = refs
        h = h_ref[...]
    out = _dot(h, w_ref[...]) + bias_ref[...]
    y_ref[...] = _layer_norm(DN_ALPHA * x_ref[...] + out, g_ref[...], b_ref[...])


def _proj_norm(x, h, gate, w, bias, ln_g, ln_b, *, tm):
    M, D = x.shape
    K = h.shape[1]
    tm = min(tm, M)
    assert M % tm == 0
    row = lambda c: pl.BlockSpec((tm, c), lambda i: (i, 0))
    const = lambda r, c: pl.BlockSpec((r, c), lambda i: (0, 0))
    operands = [x, h] + ([gate] if gate is not None else []) + [w, bias.reshape(1, D), ln_g.reshape(1, D), ln_b.reshape(1, D)]
    specs = [row(D), row(K)] + ([row(K)] if gate is not None else []) + [const(K, D), const(1, D), const(1, D), const(1, D)]
    return pl.pallas_call(
        functools.partial(_proj_norm_body, gate is not None),
        grid=(M // tm,),
        in_specs=specs,
        out_specs=row(D),
        out_shape=jax.ShapeDtypeStruct((M, D), F32),
        compiler_params=_params(("arbitrary",)),
        name="proj_norm",
    )(*operands)


def _softplus(z):
    return jnp.maximum(z, 0.0) + jnp.log(1.0 + jnp.exp(-jnp.abs(z)))


def _rwkv_proj_body(seq_len, tm, x_ref, halo_ref, mu_ref, wrkv_ref, w1_ref, w2_ref, a1_ref, a2_ref,
                    g1_ref, g2_ref, vec_ref, r_ref, k_ref, v_ref, d_ref, a_ref, g_ref):
    i = pl.program_id(0)
    x = x_ref[...]
    D = x.shape[-1]
    rows = lax.broadcasted_iota(jnp.int32, (tm, 1), 0)
    if seq_len >= tm:
        starts_sequence = (i * tm) % seq_len == 0
        before = jnp.where(starts_sequence, 0.0, halo_ref[SUBLANES - 1:SUBLANES, :])
        pos = rows
    else:
        nseq = tm // seq_len
        before = jnp.broadcast_to(halo_ref[...], (nseq, seq_len, D)).reshape(tm, D)
        pos = rows % seq_len
    dx = _prev_rows(x, 1, pos, [before]) - x
    mu = mu_ref[...]
    mix = lambda j: x + dx * mu[j:j + 1]
    vec = vec_ref[...]
    r_ref[...] = _dot(mix(0), wrkv_ref[0])
    k_ref[...] = _dot(mix(1), wrkv_ref[1])
    v_ref[...] = _dot(mix(2), wrkv_ref[2])
    w_pre = vec[0:1] + _dot(jnp.tanh(_dot(mix(3), w1_ref[...])), w2_ref[...])
    d_ref[...] = jnp.exp(-jnp.exp(-_softplus(-w_pre) - 0.5))
    a_ref[...] = _sigmoid(vec[1:2] + _dot(_dot(mix(4), a1_ref[...]), a2_ref[...]))
    g_ref[...] = _dot(_sigmoid(_dot(mix(5), g1_ref[...])), g2_ref[...])


def _rwkv_proj(x, shift, mu, w_rkv, w1, w2, a1, a2, g1, g2, vec, *, tm):
    B, T, D = x.shape
    M = B * T
    x2 = x.reshape(M, D)
    if shift is None:
        tm = min(tm, T)
        assert T % tm == 0
        halo = x2
        halo_spec = pl.BlockSpec((SUBLANES, D), lambda i: (jnp.maximum(i * (tm // SUBLANES) - 1, 0), 0))
    else:
        assert T == SUBLANES
        tm = min(tm, M)
        assert M % tm == 0
        halo = shift.reshape(B, 1, D)
        halo_spec = pl.BlockSpec((tm // T, 1, D), lambda i: (i, 0, 0))
    full = lambda arr: pl.BlockSpec(arr.shape, lambda i: (0,) * arr.ndim)
    row = pl.BlockSpec((tm, D), lambda i: (i, 0))
    outs = pl.pallas_call(
        functools.partial(_rwkv_proj_body, T, tm),
        grid=(M // tm,),
        in_specs=[row, halo_spec] + [full(w) for w in (mu, w_rkv, w1, w2, a1, a2, g1, g2, vec)],
        out_specs=[row] * 6,
        out_shape=[jax.ShapeDtypeStruct((M, D), F32)] * 6,
        compiler_params=_params(("arbitrary",)),
        name="rwkv_proj",
    )(x2, halo, mu, w_rkv, w1, w2, a1, a2, g1, g2, vec)
    return [o.reshape(B, T, D) for o in outs]


QUAD = 4
QUAD_LANES = QUAD * RWKV_HEAD
MXU_DEPTH = 256


def _split3(z):
    z1 = z.astype(BF16)
    r1 = z - z1.astype(F32)
    z2 = r1.astype(BF16)
    z3 = (r1 - z2.astype(F32)).astype(BF16)
    return z1, z2, z3


def _seg_sum_mxu(z, ones):
    n = z.shape[0]
    s = jnp.dot(jnp.concatenate(_split3(z), axis=0), ones, preferred_element_type=F32)
    return s[0:n] + s[n:2 * n] + s[2 * n:3 * n]


def _wkv_body(Gb, Gq, Tc, U, has_state, r_ref, k_ref, v_ref, d_ref, a_ref, vec_ref, *rest):
    if has_state:
        s0_ref, y_ref, sT_ref, S_scr, av_scr, bv_scr, dv_scr, rv_scr, vk_scr = rest
    else:
        y_ref, sT_ref, S_scr, av_scr, bv_scr, dv_scr, rv_scr, vk_scr = rest
    N, QL = RWKV_HEAD, QUAD_LANES
    ic = pl.program_id(2)
    tiles = [(gb, gq) for gb in range(Gb) for gq in range(Gq)]
    G = len(tiles)
    n_parts = 3 * QUAD * U
    KP = LANES if n_parts <= LANES else MXU_DEPTH
    headU = lax.broadcasted_iota(jnp.int32, (U, QL), 1) // N
    lane_t = lax.broadcasted_iota(jnp.int32, (N, QL), 1) % N
    lane_step = lax.broadcasted_iota(jnp.int32, (N, KP), 1) % U
    ri = lax.broadcasted_iota(jnp.int32, (QL, QL), 0) // N
    ci = lax.broadcasted_iota(jnp.int32, (QL, QL), 1) // N
    ones = jnp.where(ri == ci, 1.0, 0.0).astype(BF16)
    pad_rows = jnp.zeros((KP - n_parts, QL), F32)

    @pl.when(ic == 0)
    def _():
        for c, (gb, gq) in enumerate(tiles):
            if has_state:
                S_scr[c] = jnp.concatenate([s0_ref[gb, QUAD * gq + h] for h in range(QUAD)], axis=-1)
            else:
                S_scr[c] = jnp.zeros((N, QL), F32)

    def head_rows(z):
        return [jnp.where(headU == h, z, 0.0) for h in range(QUAD)]

    def sub_chunk(j, carry):
        off = pl.multiple_of(j * U, U)
        S = []
        for c, (gb, gq) in enumerate(tiles):
            lanes = slice(QL * gq, QL * (gq + 1))
            sl = (gb, pl.ds(off, U), lanes)
            r, k, v, a = r_ref[sl], k_ref[sl], v_ref[sl], a_ref[sl]
            vec = vec_ref[:, lanes]
            kk = k * vec[0:1]
            kk = kk / jnp.maximum(jnp.sqrt(_seg_sum_mxu(kk * kk, ones)), 1e-12)
            kh = k * (1.0 + (a - 1.0) * vec[1:2])
            av_scr[c] = -kk
            bv_scr[c] = kk * a
            dv_scr[c] = d_ref[sl]
            rv_scr[c] = r
            v_hi = v.astype(BF16).astype(F32)
            v_lo = v - v_hi
            k_hi = kh.astype(BF16).astype(F32)
            k_lo = kh - k_hi
            v_stack = jnp.concatenate(head_rows(v_hi) + head_rows(v_hi) + head_rows(v_lo) + [pad_rows], axis=0)
            v_t = v_stack.T
            v_cols = functools.reduce(jnp.add, [v_t[N * h:N * (h + 1)] for h in range(QUAD)]).astype(BF16)
            k_rows = jnp.concatenate(head_rows(k_hi) + head_rows(k_lo) + head_rows(k_hi) + [pad_rows],
                                     axis=0).astype(BF16)
            lhs = jnp.concatenate([jnp.where(lane_step == t, v_cols, jnp.zeros_like(v_cols)) for t in range(U)],
                                  axis=0)
            vk_scr[c] = jnp.dot(lhs, k_rows, preferred_element_type=F32)
            S.append(S_scr[c])
        y_cols = [jnp.zeros((N, QL), F32) for _ in tiles]
        for t in range(U + 1):
            blocks = []
            for c in range(G):
                if t < U:
                    blocks.append((S[c] * av_scr[c, t:t + 1, :]).astype(BF16))
                if t > 0:
                    blocks.append((S[c] * rv_scr[c, t - 1:t, :]).astype(BF16))
            out = jnp.dot(jnp.concatenate(blocks, axis=0), ones, preferred_element_type=F32)
            per = N * ((t < U) + (t > 0))
            for c in range(G):
                base = c * per
                if t > 0:
                    y_cols[c] = jnp.where(lane_t == t - 1, out[base + per - N:base + per], y_cols[c])
                if t < U:
                    sa = out[base:base + N]
                    S[c] = S[c] * dv_scr[c, t:t + 1, :] + sa * bv_scr[c, t:t + 1, :] + vk_scr[c, N * t:N * (t + 1), :]
        for c, (gb, gq) in enumerate(tiles):
            lanes = slice(QL * gq, QL * (gq + 1))
            sl = (gb, pl.ds(off, U), lanes)
            S_scr[c] = S[c]
            vec = vec_ref[:, lanes]
            r, k, v, a = r_ref[sl], k_ref[sl], v_ref[sl], a_ref[sl]
            kh = k * (1.0 + (a - 1.0) * vec[1:2])
            bonus = _seg_sum_mxu(r * kh * vec[4:5], ones) * v
            y_t = y_cols[c].T
            y = jnp.concatenate([y_t[N * h:N * h + U, :] for h in range(QUAD)], axis=-1)
            mean = _seg_sum_mxu(y, ones) * (1.0 / N)
            yc = y - mean
            var = _seg_sum_mxu(yc * yc, ones) * (1.0 / N)
            y_ref[sl] = yc * lax.rsqrt(var + GN_EPS) * vec[2:3] + vec[3:4] + bonus
        return carry

    lax.fori_loop(0, Tc // U, sub_chunk, 0)

    @pl.when(ic == pl.num_programs(2) - 1)
    def _():
        for c, (gb, gq) in enumerate(tiles):
            Sc = S_scr[c]
            for h in range(QUAD):
                sT_ref[gb, QUAD * gq + h] = Sc[:, N * h:N * (h + 1)]


def _wkv(r, k, v, d, a, vec8, state, *, Gb, Gq, Tc, U):
    B, T, D = r.shape
    H, N, QL = RWKV_HEADS, RWKV_HEAD, QUAD_LANES
    Tc = min(Tc, T)
    U = min(U, Tc)
    nq = H // QUAD
    assert 3 * QUAD * U <= MXU_DEPTH and U % SUBLANES == 0 and U <= N
    assert T % Tc == 0 and Tc % U == 0 and B % Gb == 0 and nq % Gq == 0
    G = Gb * Gq
    seq = pl.BlockSpec((Gb, Tc, QL * Gq), lambda b, p, c: (b, c, p))
    st = pl.BlockSpec((Gb, QUAD * Gq, N, N), lambda b, p, c: (b, p, 0, 0))
    operands = [r, k, v, d, a, vec8] + ([state] if state is not None else [])
    specs = [seq] * 5 + [pl.BlockSpec((8, QL * Gq), lambda b, p, c: (0, p))] + ([st] if state is not None else [])
    rows = pltpu.VMEM((G, U, QL), F32)
    return pl.pallas_call(
        functools.partial(_wkv_body, Gb, Gq, Tc, U, state is not None),
        grid=(B // Gb, nq // Gq, T // Tc),
        in_specs=specs,
        out_specs=[seq, st],
        out_shape=[jax.ShapeDtypeStruct((B, T, D), F32), jax.ShapeDtypeStruct((B, H, N, N), F32)],
        scratch_shapes=[pltpu.VMEM((G, N, QL), F32), rows, rows, rows, rows, pltpu.VMEM((G, U * N, QL), F32)],
        compiler_params=_params(("arbitrary", "arbitrary", "arbitrary")),
        name="wkv",
    )(*operands)


def _rwkv_layer(x, shift, wkv_state, prm, ln_g, ln_b, *, tm, wkv_cfg):
    mu, w_rkv, w1, w2, a1, a2, g1, g2, vec, r_k, w_o = prm
    B, T, D = x.shape
    r, k, v, d, a, g = _rwkv_proj(x, shift, mu, w_rkv, w1, w2, a1, a2, g1, g2, vec, tm=tm)
    vec8 = jnp.concatenate([vec[2:6], r_k.reshape(1, D), jnp.zeros((3, D), F32)], axis=0)
    y, s_new = _wkv(r, k, v, d, a, vec8, wkv_state, **wkv_cfg)
    x_new = _proj_norm(x.reshape(B * T, D), y.reshape(B * T, D), g.reshape(B * T, D), w_o,
                       jnp.zeros((D,), F32), ln_g, ln_b, tm=tm)
    return x_new.reshape(B, T, D), s_new, x[:, -1]


CONV_HIST = CONV_WIDTH - 1
CONV_HALO = 32


def _glu_body(x_ref, w_ref, b_ref, u_ref):
    h = _dot(x_ref[...], w_ref[...]) + b_ref[...]
    D = u_ref.shape[-1]
    u_ref[...] = h[:, :D] * _sigmoid(h[:, D:])


def _glu(x2, w, b, *, tm):
    M, D = x2.shape
    tm = min(tm, M)
    assert M % tm == 0
    return pl.pallas_call(
        _glu_body,
        grid=(M // tm,),
        in_specs=[pl.BlockSpec((tm, D), lambda i: (i, 0)),
                  pl.BlockSpec((D, 2 * D), lambda i: (0, 0)),
                  pl.BlockSpec((1, 2 * D), lambda i: (0, 0))],
        out_specs=pl.BlockSpec((tm, D), lambda i: (i, 0)),
        out_shape=jax.ShapeDtypeStruct((M, D), F32),
        compiler_params=_params(("arbitrary",)),
        name="conf_glu",
    )(x2, w, b.reshape(1, 2 * D))


def _conf_tail(conv, x, bdw_ref, cln_ref, w2_ref, b2_ref, g_ref, b_ref):
    cln = cln_ref[...]
    c = _silu(_layer_norm(conv + bdw_ref[...], cln[0:1], cln[1:2]))
    out = _dot(c, w2_ref[...]) + b2_ref[...]
    return _layer_norm(DN_ALPHA * x + out, g_ref[...], b_ref[...])


def _conf_long_body(seq_len, tm, x_ref, u_ref, halo_ref, wdw_ref, bdw_ref, cln_ref, w2_ref, b2_ref, g_ref, b_ref,
                    y_ref, ext_ref):
    i = pl.program_id(0)
    starts_sequence = (i * tm) % seq_len == 0
    ext_ref[0:CONV_HALO, :] = jnp.where(starts_sequence, 0.0, halo_ref[...])
    ext_ref[CONV_HALO:CONV_HALO + tm, :] = u_ref[...]
    wdw = wdw_ref[...]
    first = CONV_HALO - CONV_HIST
    conv = ext_ref[first:first + tm, :] * wdw[0:1]
    for j in range(1, CONV_WIDTH):
        conv = conv + ext_ref[first + j:first + j + tm, :] * wdw[j:j + 1]
    y_ref[...] = _conf_tail(conv, x_ref[...], bdw_ref, cln_ref, w2_ref, b2_ref, g_ref, b_ref)


def _conf_short_body(seq_len, nb, x_ref, u_ref, hist_ref, wdw_ref, bdw_ref, cln_ref, w2_ref, b2_ref, g_ref, b_ref,
                     y_ref, hist_out_ref, ext_ref):
    D = u_ref.shape[-1]
    ext_ref[:, 0:CONV_HIST, :] = hist_ref[...]
    ext_ref[:, CONV_HIST:CONV_HIST + seq_len, :] = u_ref[...].reshape(nb, seq_len, D)
    wdw = wdw_ref[...]
    conv = ext_ref[:, 0:seq_len, :] * wdw[0:1]
    for j in range(1, CONV_WIDTH):
        conv = conv + ext_ref[:, j:j + seq_len, :] * wdw[j:j + 1]
    hist_out_ref[...] = ext_ref[:, seq_len:seq_len + CONV_HIST, :]
    y_ref[...] = _conf_tail(conv.reshape(nb * seq_len, D), x_ref[...], bdw_ref, cln_ref, w2_ref, b2_ref, g_ref, b_ref)


def _conformer_layer(x, hist, prm, ln_g, ln_b, *, tm):
    w_pw1, b_pw1, w_dw, b_dw, cln, w_pw2, b_pw2 = prm
    B, T, D = x.shape
    M = B * T
    x2 = x.reshape(M, D)
    u = _glu(x2, w_pw1, b_pw1, tm=tm)
    const = lambda r, c: pl.BlockSpec((r, c), lambda i: (0, 0))
    weights = [w_dw, b_dw.reshape(1, D), cln, w_pw2, b_pw2.reshape(1, D), ln_g.reshape(1, D), ln_b.reshape(1, D)]
    wspecs = [const(CONV_WIDTH, D), const(1, D), const(2, D), const(D, D), const(1, D), const(1, D), const(1, D)]
    if hist is None:
        tm = min(tm, T)
        assert T % tm == 0 and tm % CONV_HALO == 0
        row = pl.BlockSpec((tm, D), lambda i: (i, 0))
        halo_spec = pl.BlockSpec((CONV_HALO, D), lambda i: (jnp.maximum(i * (tm // CONV_HALO) - 1, 0), 0))
        y = pl.pallas_call(
            functools.partial(_conf_long_body, T, tm),
            grid=(M // tm,),
            in_specs=[row, row, halo_spec] + wspecs,
            out_specs=row,
            out_shape=jax.ShapeDtypeStruct((M, D), F32),
            scratch_shapes=[pltpu.VMEM((CONV_HALO + tm, D), F32)],
            compiler_params=_params(("arbitrary",)),
            name="conf_conv_long",
        )(x2, u, u, *weights)
        new_hist = u.reshape(B, T, D)[:, T - CONV_HIST:, :]
    else:
        assert T == SUBLANES
        nb = min(tm // T, B)
        assert B % nb == 0
        row = pl.BlockSpec((nb * T, D), lambda i: (i, 0))
        hspec = pl.BlockSpec((nb, CONV_HIST, D), lambda i: (i, 0, 0))
        y, new_hist = pl.pallas_call(
            functools.partial(_conf_short_body, T, nb),
            grid=(B // nb,),
            in_specs=[row, row, hspec] + wspecs,
            out_specs=[row, hspec],
            out_shape=[jax.ShapeDtypeStruct((M, D), F32), jax.ShapeDtypeStruct((B, CONV_HIST, D), F32)],
            scratch_shapes=[pltpu.VMEM((nb, CONV_HIST + T + 2, D), F32)],
            compiler_params=_params(("arbitrary",)),
            name="conf_conv_short",
        )(x2, u, hist, *weights)
    return y.reshape(B, T, D), new_hist


def _rms_norm(z, g):
    return z * lax.rsqrt(jnp.mean(z * z, axis=-1, keepdims=True) + RMS_EPS) * g


def _mla_proj_body(keys_bf16, x_ref, cos_ref, sin_ref, wdq_ref, qn_ref, wqn_ref, wqr_ref, wqs_ref, wckv_ref, wkr_ref,
                   wks_ref, kvn_ref, wuk_ref, qlat_ref, qpe_ref, ckv_ref, kpe_ref, *key_copies):
    x = x_ref[...]
    cos, sin = cos_ref[...], sin_ref[...]
    cq = _rms_norm(_dot(x, wdq_ref[...]), qn_ref[...])
    q_nope = _dot(cq, wqn_ref[...])
    q_rope = _dot(cq, wqr_ref[...])
    q_swap = _dot(cq, wqs_ref[...])
    for h in range(MLA_HEADS):
        q_lat = _dot(q_nope[:, h * QK_NOPE:(h + 1) * QK_NOPE], wuk_ref[h])
        qlat_ref[h] = (q_lat * MLA_SCALE).astype(qlat_ref.dtype)
        sl = slice(h * QK_ROPE, (h + 1) * QK_ROPE)
        qpe_ref[h] = ((q_rope[:, sl] * cos + q_swap[:, sl] * sin) * MLA_SCALE).astype(qpe_ref.dtype)
    ckv = _rms_norm(_dot(x, wckv_ref[...]), kvn_ref[...])
    kpe = _dot(x, wkr_ref[...]) * cos + _dot(x, wks_ref[...]) * sin
    ckv_ref[...] = ckv
    kpe_ref[...] = kpe
    if keys_bf16:
        key_copies[0][...] = ckv.astype(BF16)
        key_copies[1][...] = kpe.astype(BF16)


def _swap_halves(w):
    half = w.shape[-1] // 2
    return jnp.concatenate([w[..., half:], w[..., :half]], axis=-1)


def _rope_tables(pos):
    half = QK_ROPE // 2
    inv = ROPE_THETA ** (-jnp.arange(half, dtype=F32) / half)
    ang = pos[:, None] * inv[None, :]
    cos, sin = jnp.cos(ang), jnp.sin(ang)
    return jnp.concatenate([cos, cos], -1), jnp.concatenate([-sin, sin], -1)


def _mla_proj(x2, pos, w_dq, q_norm, w_uq, w_dkv, kv_norm, w_uk, *, tm, keys_bf16):
    M, D = x2.shape
    H = MLA_HEADS
    tm = min(tm, M)
    assert M % tm == 0
    cos, sin = _rope_tables(pos)
    w_qn = w_uq[:, :, :QK_NOPE].reshape(Q_LORA, H * QK_NOPE)
    w_qr = w_uq[:, :, QK_NOPE:].reshape(Q_LORA, H * QK_ROPE)
    w_qs = _swap_halves(w_uq[:, :, QK_NOPE:]).reshape(Q_LORA, H * QK_ROPE)
    w_ckv, w_kr = w_dkv[:, :KV_LORA], w_dkv[:, KV_LORA:]
    w_ks = _swap_halves(w_kr)
    w_ukT = jnp.transpose(w_uk, (1, 2, 0))
    full = lambda arr: pl.BlockSpec(arr.shape, lambda i: (0,) * arr.ndim)
    row = lambda c: pl.BlockSpec((tm, c), lambda i: (i, 0))
    hrow = lambda c: pl.BlockSpec((H, tm, c), lambda i: (0, i, 0))
    weights = [w_dq, q_norm.reshape(1, Q_LORA), w_qn, w_qr, w_qs, w_ckv, w_kr, w_ks, kv_norm.reshape(1, KV_LORA), w_ukT]
    qdt = BF16 if keys_bf16 else F32
    out_specs = [hrow(KV_LORA), hrow(QK_ROPE), row(KV_LORA), row(QK_ROPE)]
    out_shape = [jax.ShapeDtypeStruct((H, M, KV_LORA), qdt), jax.ShapeDtypeStruct((H, M, QK_ROPE), qdt),
                 jax.ShapeDtypeStruct((M, KV_LORA), F32), jax.ShapeDtypeStruct((M, QK_ROPE), F32)]
    if keys_bf16:
        out_specs += [row(KV_LORA), row(QK_ROPE)]
        out_shape += [jax.ShapeDtypeStruct((M, KV_LORA), BF16), jax.ShapeDtypeStruct((M, QK_ROPE), BF16)]
    return pl.pallas_call(
        functools.partial(_mla_proj_body, keys_bf16),
        grid=(M // tm,),
        in_specs=[row(D), row(QK_ROPE), row(QK_ROPE)] + [full(w) for w in weights],
        out_specs=out_specs,
        out_shape=out_shape,
        compiler_params=_params(("arbitrary",)),
        name="mla_proj",
    )(x2, cos, sin, *weights)


def _scores(ql, qp, ckv, kpe):
    return _dot_t(ql, ckv) + _dot_t(qp, kpe)


def _softmax_step(s, values, m_ref, l_ref, acc_ref):
    m_old = m_ref[...]
    m_new = jnp.maximum(m_old, jnp.max(s, axis=-1, keepdims=True))
    p = jnp.exp(s - m_new)
    alpha = jnp.exp(m_old - m_new)
    l_ref[...] = alpha * l_ref[...] + jnp.sum(p, axis=-1, keepdims=True)
    pv = functools.reduce(jnp.add, [_dot(p[:, ks], v) for ks, v in values])
    acc_ref[...] = alpha * acc_ref[...] + pv
    m_ref[...] = m_new


HEAD_GROUP = 2


def _flash_body(tq, tk, qi_ref, kj_ref, qlat_ref, qpe_ref, ckv_ref, kpe_ref, o_ref, m_ref, l_ref, acc_ref):
    n = pl.program_id(1)
    i, j = qi_ref[n], kj_ref[n]
    H = MLA_HEADS
    last_j = ((i + 1) * tq - 1) // tk

    @pl.when(j == 0)
    def _():
        m_ref[...] = jnp.full(m_ref.shape, -jnp.inf, F32)
        l_ref[...] = jnp.zeros(l_ref.shape, F32)
        acc_ref[...] = jnp.zeros(acc_ref.shape, F32)

    def update(masked):
        ckv, kpe = ckv_ref[...], kpe_ref[...]
        for h0 in range(0, H, HEAD_GROUP):
            hs = slice(h0, h0 + HEAD_GROUP)
            rows = slice(h0 * tq, (h0 + HEAD_GROUP) * tq)
            s = _scores(qlat_ref[hs].reshape(HEAD_GROUP * tq, KV_LORA), qpe_ref[hs].reshape(HEAD_GROUP * tq, QK_ROPE),
                        ckv, kpe)
            if masked:
                q_pos = i * tq + lax.broadcasted_iota(jnp.int32, (HEAD_GROUP * tq, 1), 0) % tq
                k_pos = j * tk + lax.broadcasted_iota(jnp.int32, (1, tk), 1)
                s = jnp.where(k_pos <= q_pos, s, -jnp.inf)
            _softmax_step(s, [(slice(None), ckv)], m_ref.at[rows], l_ref.at[rows], acc_ref.at[rows])

    @pl.when(j < last_j)
    def _():
        update(False)

    @pl.when(j == last_j)
    def _():
        update(True)
        o_ref[...] = (acc_ref[...] / l_ref[...]).reshape(H, tq, KV_LORA)


def _flash_attention(q_lat, q_pe, ckv, kpe, B, T, *, tq, tk):
    H = MLA_HEADS
    tq, tk = min(tq, T), min(tk, T)
    assert T % tq == 0 and T % tk == 0 and tk % tq == 0
    nq, nk = T // tq, T // tk
    pairs = [(i, j) for i in range(nq) for j in range(((i + 1) * tq - 1) // tk + 1)]
    qi = jnp.asarray([p[0] for p in pairs], jnp.int32)
    kj = jnp.asarray([p[1] for p in pairs], jnp.int32)
    qmap = lambda b, n, qi, kj: (0, b * nq + qi[n], 0)
    kmap = lambda b, n, qi, kj: (b * nk + kj[n], 0)
    grid_spec = pltpu.PrefetchScalarGridSpec(
        num_scalar_prefetch=2,
        grid=(B, len(pairs)),
        in_specs=[pl.BlockSpec((H, tq, KV_LORA), qmap), pl.BlockSpec((H, tq, QK_ROPE), qmap),
                  pl.BlockSpec((tk, KV_LORA), kmap), pl.BlockSpec((tk, QK_ROPE), kmap)],
        out_specs=pl.BlockSpec((H, tq, KV_LORA), qmap),
        scratch_shapes=[pltpu.VMEM((H * tq, 1), F32), pltpu.VMEM((H * tq, 1), F32), pltpu.VMEM((H * tq, KV_LORA), F32)],
    )
    return pl.pallas_call(
        functools.partial(_flash_body, tq, tk),
        grid_spec=grid_spec,
        out_shape=jax.ShapeDtypeStruct((H, B * T, KV_LORA), F32),
        compiler_params=_params(("arbitrary", "arbitrary")),
        name="mla_flash",
    )(qi, kj, q_lat, q_pe, ckv, kpe)


def _paged_body(PG, T, page, pt_ref, qlat_ref, qpe_ref, ckvn_ref, kpen_ref, *rest):
    pages_ckv, pages_kpe = rest[:PG], rest[PG:2 * PG]
    o_ref, m_ref, l_ref, acc_ref, selfk_ref, selfp_ref = rest[2 * PG:]
    g = pl.program_id(1)
    H = MLA_HEADS
    ql = qlat_ref[...].reshape(H * T, KV_LORA).astype(BF16)
    qp = qpe_ref[...].reshape(H * T, QK_ROPE).astype(BF16)

    @pl.when(g == 0)
    def _():
        m_ref[...] = jnp.full(m_ref.shape, -jnp.inf, F32)
        l_ref[...] = jnp.zeros(l_ref.shape, F32)
        acc_ref[...] = jnp.zeros(acc_ref.shape, F32)

    keys = jnp.concatenate([pages_ckv[p][0, 0].astype(BF16) for p in range(PG)], axis=0)
    rope_keys = jnp.concatenate([pages_kpe[p][0, 0].astype(BF16) for p in range(PG)], axis=0)
    _softmax_step(_scores(ql, qp, keys, rope_keys), [(slice(None), keys)], m_ref, l_ref, acc_ref)

    @pl.when(g == pl.num_programs(1) - 1)
    def _():
        selfk_ref[...] = jnp.zeros(selfk_ref.shape, F32)
        selfp_ref[...] = jnp.zeros(selfp_ref.shape, F32)
        selfk_ref[0:T, :] = ckvn_ref[...]
        selfp_ref[0:T, :] = kpen_ref[...]
        ckv = selfk_ref[...]
        s = _scores(ql, qp, ckv, selfp_ref[...])
        q_pos = lax.broadcasted_iota(jnp.int32, (H * T, 1), 0) % T
        k_pos = lax.broadcasted_iota(jnp.int32, (1, page), 1)
        s = jnp.where(k_pos <= q_pos, s, -jnp.inf)
        _softmax_step(s, [(slice(None), ckv)], m_ref, l_ref, acc_ref)
        o_ref[...] = (acc_ref[...] / l_ref[...]).reshape(H, T, KV_LORA)


def _paged_attention(q_lat, q_pe, ckv, kpe, cache_ckv, cache_kpe, page_table, layer, T, *, PG):
    H = MLA_HEADS
    DB, n_pages = page_table.shape
    page = cache_ckv.shape[2]
    assert n_pages % PG == 0 and T <= page
    qmap = lambda b, g, pt: (0, b, 0)
    nmap = lambda b, g, pt: (b, 0)
    pmap = lambda p: (lambda b, g, pt: (layer, pt[b * n_pages + g * PG + p], 0, 0))
    grid_spec = pltpu.PrefetchScalarGridSpec(
        num_scalar_prefetch=1,
        grid=(DB, n_pages // PG),
        in_specs=[pl.BlockSpec((H, T, KV_LORA), qmap), pl.BlockSpec((H, T, QK_ROPE), qmap),
                  pl.BlockSpec((T, KV_LORA), nmap), pl.BlockSpec((T, QK_ROPE), nmap)]
                 + [pl.BlockSpec((1, 1, page, KV_LORA), pmap(p)) for p in range(PG)]
                 + [pl.BlockSpec((1, 1, page, QK_ROPE), pmap(p)) for p in range(PG)],
        out_specs=pl.BlockSpec((H, T, KV_LORA), qmap),
        scratch_shapes=[pltpu.VMEM((H * T, 1), F32), pltpu.VMEM((H * T, 1), F32), pltpu.VMEM((H * T, KV_LORA), F32),
                        pltpu.VMEM((page, KV_LORA), F32), pltpu.VMEM((page, QK_ROPE), F32)],
    )
    return pl.pallas_call(
        functools.partial(_paged_body, PG, T, page),
        grid_spec=grid_spec,
        out_shape=jax.ShapeDtypeStruct((H, DB * T, KV_LORA), F32),
        compiler_params=_params(("arbitrary", "arbitrary")),
        name="mla_paged",
    )(page_table.reshape(-1), q_lat, q_pe, ckv, kpe, *([cache_ckv] * PG), *([cache_kpe] * PG))


def _mla_out_body(x_ref, o_ref, wuv_ref, wo_ref, g_ref, b_ref, y_ref):
    heads = [_dot(o_ref[h], wuv_ref[h]) for h in range(MLA_HEADS)]
    out = _dot(jnp.concatenate(heads, axis=-1), wo_ref[...])
    y_ref[...] = _layer_norm(DN_ALPHA * x_ref[...] + out, g_ref[...], b_ref[...])


def _mla_out(x2, o_lat, w_uv, w_o, ln_g, ln_b, *, tm):
    M, D = x2.shape
    H = MLA_HEADS
    tm = min(tm, M)
    assert M % tm == 0
    w_uvh = jnp.transpose(w_uv, (1, 0, 2))
    const = lambda shape: pl.BlockSpec(shape, lambda i: (0,) * len(shape))
    return pl.pallas_call(
        _mla_out_body,
        grid=(M // tm,),
        in_specs=[pl.BlockSpec((tm, D), lambda i: (i, 0)), pl.BlockSpec((H, tm, KV_LORA), lambda i: (0, i, 0)),
                  const(w_uvh.shape), const(w_o.shape), const((1, D)), const((1, D))],
        out_specs=pl.BlockSpec((tm, D), lambda i: (i, 0)),
        out_shape=jax.ShapeDtypeStruct((M, D), F32),
        compiler_params=_params(("arbitrary",)),
        name="mla_out",
    )(x2, o_lat, w_uvh, w_o, ln_g.reshape(1, D), ln_b.reshape(1, D))


def _mla_layer(x, pos0, cache, prm, ln_g, ln_b, *, tm, attn_cfg):
    w_dq, q_norm, w_uq, w_dkv, kv_norm, w_uk, w_uv, w_o = prm
    B, T, D = x.shape
    x2 = x.reshape(B * T, D)
    pos = jnp.tile(pos0 + jnp.arange(T, dtype=F32), B)
    proj = _mla_proj(x2, pos, w_dq, q_norm, w_uq, w_dkv, kv_norm, w_uk, tm=tm, keys_bf16=cache is None)
    q_lat, q_pe, ckv, kpe = proj[:4]
    if cache is None:
        o_lat = _flash_attention(q_lat, q_pe, proj[4], proj[5], B, T, **attn_cfg)
    else:
        o_lat = _paged_attention(q_lat, q_pe, ckv, kpe, *cache, T, **attn_cfg)
    y = _mla_out(x2, o_lat, w_uv, w_o, ln_g, ln_b, tm=tm)
    return y.reshape(B, T, D), ckv.reshape(B, T, KV_LORA), kpe.reshape(B, T, QK_ROPE)


def kernel(x_prompt, x_sample, state_wkv, state_shift, state_conv, cache_mla_ckv, cache_mla_kpe, state_ffn_conv, page_table, ln_g, ln_b, rwkv_mu, rwkv_w_rkv, rwkv_w1, rwkv_w2, rwkv_a1, rwkv_a2, rwkv_g1, rwkv_g2, rwkv_vec, rwkv_r_k, rwkv_w_o, conf_w_pw1, conf_b_pw1, conf_w_dw, conf_b_dw, conf_ln, conf_w_pw2, conf_b_pw2, mla_w_dq, mla_q_norm, mla_w_uq, mla_w_dkv, mla_kv_norm, mla_w_uk, mla_w_uv, mla_w_o, ffn_w_in, ffn_w_dw, ffn_b_dw, ffn_w_out):
    bf = lambda w: w.astype(BF16)
    past_len = page_table.shape[1] * cache_mla_ckv.shape[2]
    xp, xs = x_prompt, x_sample
    out_p = dict(wkv=[], shift=[], conv=[], ckv=[], kpe=[], ffn=[])
    out_s = dict(wkv=[], shift=[], conv=[], ckv=[], kpe=[], ffn=[])
    for i in range(DEPTH):
        j, kind = i // N_MIXERS, i % N_MIXERS
        g0, b0, g1, b1 = ln_g[i, 0], ln_b[i, 0], ln_g[i, 1], ln_b[i, 1]
        if kind == 0:
            prm = (rwkv_mu[j], bf(rwkv_w_rkv[j]), bf(rwkv_w1[j]), bf(rwkv_w2[j]), bf(rwkv_a1[j]), bf(rwkv_a2[j]),
                   bf(rwkv_g1[j]), bf(rwkv_g2[j]), rwkv_vec[j], rwkv_r_k[j], bf(rwkv_w_o[j]))
            xp, st, sh = _rwkv_layer(xp, None, None, prm, g0, b0, tm=512, wkv_cfg=dict(Gb=2, Gq=4, Tc=128, U=16))
            out_p["wkv"].append(st)
            out_p["shift"].append(sh)
            xs, st, sh = _rwkv_layer(xs, state_shift[j], state_wkv[j], prm, g0, b0, tm=512,
                                     wkv_cfg=dict(Gb=2, Gq=4, Tc=8, U=8))
            out_s["wkv"].append(st)
            out_s["shift"].append(sh)
        elif kind == 1:
            prm = (bf(conf_w_pw1[j]), conf_b_pw1[j], conf_w_dw[j], conf_b_dw[j], conf_ln[j], bf(conf_w_pw2[j]),
                   conf_b_pw2[j])
            xp, cb = _conformer_layer(xp, None, prm, g0, b0, tm=256)
            out_p["conv"].append(cb)
            xs, cb = _conformer_layer(xs, state_conv[j], prm, g0, b0, tm=256)
            out_s["conv"].append(cb)
        else:
            prm = (bf(mla_w_dq[j]), mla_q_norm[j], bf(mla_w_uq[j]), bf(mla_w_dkv[j]), mla_kv_norm[j],
                   bf(mla_w_uk[j]), bf(mla_w_uv[j]), bf(mla_w_o[j]))
            xp, ckv, kpe = _mla_layer(xp, 0.0, None, prm, g0, b0, tm=512, attn_cfg=dict(tq=128, tk=1024))
            out_p["ckv"].append(ckv)
            out_p["kpe"].append(kpe)
            xs, ckv, kpe = _mla_layer(xs, float(past_len), (cache_mla_ckv, cache_mla_kpe, page_table, j), prm,
                                      g0, b0, tm=512, attn_cfg=dict(PG=16))
            out_s["ckv"].append(ckv)
            out_s["kpe"].append(kpe)
        w_in, w_out = bf(ffn_w_in[i]), bf(ffn_w_out[i])
        xp, fb = _conv_ffn(xp, None, w_in, ffn_w_dw[i], ffn_b_dw[i], w_out, g1, b1, tm=512, fc=1408)
        out_p["ffn"].append(fb)
        xs, fb = _conv_ffn(xs, state_ffn_conv[i], w_in, ffn_w_dw[i], ffn_b_dw[i], w_out, g1, b1, tm=512, fc=1408)
        out_s["ffn"].append(fb)
    names = ("wkv", "shift", "conv", "ckv", "kpe", "ffn")
    return (xp, xs) + tuple(jnp.stack(out_p[n]) for n in names) + tuple(jnp.stack(out_s[n]) for n in names)
```

```python
import functools

import jax
import jax.numpy as jnp
from jax import lax
from jax.experimental import pallas as pl
from jax.experimental.pallas import tpu as pltpu

D_MODEL = 1024
DEPTH = 4
N_MIXERS = 3
DN_ALPHA = (2 * DEPTH) ** 0.25
LN_EPS = 1e-5
RWKV_HEAD = 64
RWKV_HEADS = D_MODEL // RWKV_HEAD
GN_EPS = 64e-5
CONV_WIDTH = 31
MLA_HEADS = 8
QK_NOPE = 128
QK_ROPE = 64
V_HEAD = 128
KV_LORA = 256
Q_LORA = 384
ROPE_THETA = 10000.0
MLA_SCALE = (QK_NOPE + QK_ROPE) ** -0.5
RMS_EPS = 1e-6
D_FF = 2816
FFN_CONV_WIDTH = 3

SUBLANES = 8
LANES = 128
VMEM_LIMIT = 56 * 1024 * 1024

BF16 = jnp.bfloat16
F32 = jnp.float32


def _params(sem):
    return pltpu.CompilerParams(dimension_semantics=sem, vmem_limit_bytes=VMEM_LIMIT)


def _dot(a, b):
    return jnp.dot(a.astype(BF16), b.astype(BF16), preferred_element_type=F32)


def _dot_t(a, b):
    return lax.dot_general(a.astype(BF16), b.astype(BF16), (((1,), (1,)), ((), ())),
                           preferred_element_type=F32)


def _layer_norm(z, g, b, eps=LN_EPS):
    mu = jnp.mean(z, axis=-1, keepdims=True)
    zc = z - mu
    var = jnp.mean(zc * zc, axis=-1, keepdims=True)
    return zc * lax.rsqrt(var + eps) * g + b


def _sigmoid(z):
    return 1.0 / (1.0 + jnp.exp(-z))


def _silu(z):
    return z * _sigmoid(z)


def _prev_rows(cur, shift, pos, fill):
    out = pltpu.roll(cur, shift, 0)
    for p in range(shift):
        out = jnp.where(pos == p, fill[shift - 1 - p], out)
    return out


def _ffn_body(seq_len, tm, x_ref, halo_ref, wa_ref, wb_ref, wdw_ref, bdw_ref, wo_ref, g_ref, b_ref,
              y_ref, tail_ref, acc_ref, xb_ref):
    i, f = pl.program_id(0), pl.program_id(1)

    @pl.when(f == 0)
    def _():
        xb_ref[...] = x_ref[...].astype(BF16)

    xb = xb_ref[...]
    a = _dot(xb, wa_ref[...])
    b = _dot(xb, wb_ref[...])
    fc = a.shape[-1]
    rows = lax.broadcasted_iota(jnp.int32, (tm, 1), 0)
    if seq_len >= tm:
        starts_sequence = (i * tm) % seq_len == 0
        ah = _dot(halo_ref[...], wa_ref[...])
        ah = jnp.where(starts_sequence, 0.0, ah)
        pos = rows
        before1, before2 = ah[SUBLANES - 1:SUBLANES], ah[SUBLANES - 2:SUBLANES - 1]
    else:
        hist = halo_ref[...]
        nseq = tm // seq_len
        expand = lambda r: jnp.broadcast_to(r, (nseq, seq_len, fc)).reshape(tm, fc)
        pos = rows % seq_len
        before1, before2 = expand(hist[:, 1:2, :]), expand(hist[:, 0:1, :])
    a1 = _prev_rows(a, 1, pos, [before1])
    a2 = _prev_rows(a, 2, pos, [before1, before2])
    wdw = wdw_ref[...]
    c = a2 * wdw[0:1] + a1 * wdw[1:2] + a * wdw[2:3] + bdw_ref[...]
    part = _dot(_silu(c) * b, wo_ref[...])

    @pl.when(f == 0)
    def _():
        acc_ref[...] = part

    @pl.when(f > 0)
    def _():
        acc_ref[...] += part

    if seq_len >= tm:
        tail_ref[0] = a[tm - SUBLANES:tm]
    else:
        tail_ref[...] = a.reshape(tm // seq_len, seq_len, fc)

    @pl.when(f == pl.num_programs(1) - 1)
    def _():
        y_ref[...] = _layer_norm(DN_ALPHA * x_ref[...] + acc_ref[...], g_ref[...], b_ref[...])


def _conv_ffn(x, hist, w_in, w_dw, b_dw, w_out, ln_g, ln_b, *, tm, fc):
    B, T, D = x.shape
    F = w_out.shape[0]
    M = B * T
    x2 = x.reshape(M, D)
    nF = F // fc
    long_seq = hist is None
    if long_seq:
        tm = min(tm, T)
        assert T % tm == 0 and tm % SUBLANES == 0
        halo = x2
        halo_spec = pl.BlockSpec((SUBLANES, D), lambda i, f: (jnp.maximum(i * (tm // SUBLANES) - 1, 0), 0))
        tail_shape = jax.ShapeDtypeStruct((M // tm, SUBLANES, F), F32)
        tail_spec = pl.BlockSpec((1, SUBLANES, fc), lambda i, f: (i, 0, f))
    else:
        assert T == SUBLANES
        tm = min(tm, M)
        assert M % tm == 0 and tm % T == 0
        halo = hist
        halo_spec = pl.BlockSpec((tm // T, FFN_CONV_WIDTH - 1, fc), lambda i, f: (i, 0, f))
        tail_shape = jax.ShapeDtypeStruct((B, T, F), F32)
        tail_spec = pl.BlockSpec((tm // T, T, fc), lambda i, f: (i, 0, f))
    y, tail = pl.pallas_call(
        functools.partial(_ffn_body, T, tm),
        grid=(M // tm, nF),
        in_specs=[
            pl.BlockSpec((tm, D), lambda i, f: (i, 0)),
            halo_spec,
            pl.BlockSpec((D, fc), lambda i, f: (0, f)),
            pl.BlockSpec((D, fc), lambda i, f: (0, nF + f)),
            pl.BlockSpec((FFN_CONV_WIDTH, fc), lambda i, f: (0, f)),
            pl.BlockSpec((1, fc), lambda i, f: (0, f)),
            pl.BlockSpec((fc, D), lambda i, f: (f, 0)),
            pl.BlockSpec((1, D), lambda i, f: (0, 0)),
            pl.BlockSpec((1, D), lambda i, f: (0, 0)),
        ],
        out_specs=[pl.BlockSpec((tm, D), lambda i, f: (i, 0)), tail_spec],
        out_shape=[jax.ShapeDtypeStruct((M, D), F32), tail_shape],
        scratch_shapes=[pltpu.VMEM((tm, D), F32), pltpu.VMEM((tm, D), BF16)],
        compiler_params=_params(("arbitrary", "arbitrary")),
        name="conv_ffn",
    )(x2, halo, w_in, w_in, w_dw, b_dw.reshape(1, F), w_out, ln_g.reshape(1, D), ln_b.reshape(1, D))
    if long_seq:
        tail = tail[T // tm - 1::T // tm]
    return y.reshape(B, T, D), tail[:, -(FFN_CONV_WIDTH - 1):, :]


def _proj_norm_body(gated, *refs):
    if gated:
        x_ref, h_ref, gate_ref, w_ref, bias_ref, g_ref, b_ref, y_ref = refs
        h = h_ref[...] * gate_ref[...]
    else:
        x_ref, h_ref, w_ref, bias_ref, g_ref, b_ref, y_ref = refs
        h = h_ref[...]
    out = _dot(h, w_ref[...]) + bias_ref[...]
    y_ref[...] = _layer_norm(DN_ALPHA * x_ref[...] + out, g_ref[...], b_ref[...])


def _proj_norm(x, h, gate, w, bias, ln_g, ln_b, *, tm):
    M, D = x.shape
    K = h.shape[1]
    tm = min(tm, M)
    assert M % tm == 0
    row = lambda c: pl.BlockSpec((tm, c), lambda i: (i, 0))
    const = lambda r, c: pl.BlockSpec((r, c), lambda i: (0, 0))
    operands = [x, h] + ([gate] if gate is not None else []) + [w, bias.reshape(1, D), ln_g.reshape(1, D), ln_b.reshape(1, D)]
    specs = [row(D), row(K)] + ([row(K)] if gate is not None else []) + [const(K, D), const(1, D), const(1, D), const(1, D)]
    return pl.pallas_call(
        functools.partial(_proj_norm_body, gate is not None),
        grid=(M // tm,),
        in_specs=specs,
        out_specs=row(D),
        out_shape=jax.ShapeDtypeStruct((M, D), F32),
        compiler_params=_params(("arbitrary",)),
        name="proj_norm",
    )(*operands)


def _softplus(z):
    return jnp.maximum(z, 0.0) + jnp.log(1.0 + jnp.exp(-jnp.abs(z)))


def _rwkv_proj_body(seq_len, tm, x_ref, halo_ref, mu_ref, wrkv_ref, w1_ref, w2_ref, a1_ref, a2_ref,
                    g1_ref, g2_ref, vec_ref, r_ref, k_ref, v_ref, d_ref, a_ref, g_ref):
    i = pl.program_id(0)
    x = x_ref[...]
    D = x.shape[-1]
    rows = lax.broadcasted_iota(jnp.int32, (tm, 1), 0)
    if seq_len >= tm:
        starts_sequence = (i * tm) % seq_len == 0
        before = jnp.where(starts_sequence, 0.0, halo_ref[SUBLANES - 1:SUBLANES, :])
        pos = rows
    else:
        nseq = tm // seq_len
        before = jnp.broadcast_to(halo_ref[...], (nseq, seq_len, D)).reshape(tm, D)
        pos = rows % seq_len
    dx = _prev_rows(x, 1, pos, [before]) - x
    mu = mu_ref[...]
    mix = lambda j: x + dx * mu[j:j + 1]
    vec = vec_ref[...]
    r_ref[...] = _dot(mix(0), wrkv_ref[0])
    k_ref[...] = _dot(mix(1), wrkv_ref[1])
    v_ref[...] = _dot(mix(2), wrkv_ref[2])
    w_pre = vec[0:1] + _dot(jnp.tanh(_dot(mix(3), w1_ref[...])), w2_ref[...])
    d_ref[...] = jnp.exp(-jnp.exp(-_softplus(-w_pre) - 0.5))
    a_ref[...] = _sigmoid(vec[1:2] + _dot(_dot(mix(4), a1_ref[...]), a2_ref[...]))
    g_ref[...] = _dot(_sigmoid(_dot(mix(5), g1_ref[...])), g2_ref[...])


def _rwkv_proj(x, shift, mu, w_rkv, w1, w2, a1, a2, g1, g2, vec, *, tm):
    B, T, D = x.shape
    M = B * T
    x2 = x.reshape(M, D)
    if shift is None:
        tm = min(tm, T)
        assert T % tm == 0
        halo = x2
        halo_spec = pl.BlockSpec((SUBLANES, D), lambda i: (jnp.maximum(i * (tm // SUBLANES) - 1, 0), 0))
    else:
        assert T == SUBLANES
        tm = min(tm, M)
        assert M % tm == 0
        halo = shift.reshape(B, 1, D)
        halo_spec = pl.BlockSpec((tm // T, 1, D), lambda i: (i, 0, 0))
    full = lambda arr: pl.BlockSpec(arr.shape, lambda i: (0,) * arr.ndim)
    row = pl.BlockSpec((tm, D), lambda i: (i, 0))
    outs = pl.pallas_call(
        functools.partial(_rwkv_proj_body, T, tm),
        grid=(M // tm,),
        in_specs=[row, halo_spec] + [full(w) for w in (mu, w_rkv, w1, w2, a1, a2, g1, g2, vec)],
        out_specs=[row] * 6,
        out_shape=[jax.ShapeDtypeStruct((M, D), F32)] * 6,
        compiler_params=_params(("arbitrary",)),
        name="rwkv_proj",
    )(x2, halo, mu, w_rkv, w1, w2, a1, a2, g1, g2, vec)
    return [o.reshape(B, T, D) for o in outs]


QUAD = 4
QUAD_LANES = QUAD * RWKV_HEAD
MXU_DEPTH = 256


def _split3(z):
    z1 = z.astype(BF16)
    r1 = z - z1.astype(F32)
    z2 = r1.astype(BF16)
    z3 = (r1 - z2.astype(F32)).astype(BF16)
    return z1, z2, z3


def _seg_sum_mxu(z, ones):
    n = z.shape[0]
    s = jnp.dot(jnp.concatenate(_split3(z), axis=0), ones, preferred_element_type=F32)
    return s[0:n] + s[n:2 * n] + s[2 * n:3 * n]


def _wkv_body(Gb, Gq, Tc, U, has_state, r_ref, k_ref, v_ref, d_ref, a_ref, vec_ref, *rest):
    if has_state:
        s0_ref, y_ref, sT_ref, S_scr, *bufs = rest
    else:
        y_ref, sT_ref, S_scr, *bufs = rest
    N, QL = RWKV_HEAD, QUAD_LANES
    av_scr, bv_scr, dv_scr, rv_scr, bonus_scr, vk_scr = [bufs[i::6] for i in range(6)]
    ic = pl.program_id(2)
    tiles = [(gb, gq) for gb in range(Gb) for gq in range(Gq)]
    G = len(tiles)
    n_parts = 3 * QUAD * U
    KP = LANES if n_parts <= LANES else MXU_DEPTH
    headU = lax.broadcasted_iota(jnp.int32, (U, QL), 1) // N
    lane_t = lax.broadcasted_iota(jnp.int32, (N, QL), 1) % N
    lane_step = lax.broadcasted_iota(jnp.int32, (N, KP), 1) % U
    ri = lax.broadcasted_iota(jnp.int32, (QL, QL), 0) // N
    ci = lax.broadcasted_iota(jnp.int32, (QL, QL), 1) // N
    ones = jnp.where(ri == ci, 1.0, 0.0).astype(BF16)
    pad_rows = jnp.zeros((KP - n_parts, QL), F32)

    @pl.when(ic == 0)
    def _():
        for c, (gb, gq) in enumerate(tiles):
            if has_state:
                S_scr[c] = jnp.concatenate([s0_ref[gb, QUAD * gq + h] for h in range(QUAD)], axis=-1)
            else:
                S_scr[c] = jnp.zeros((N, QL), F32)

    def head_rows(z):
        return [jnp.where(headU == h, z, 0.0) for h in range(QUAD)]

    def at(c, off):
        gb, gq = tiles[c]
        return (gb, pl.ds(off, U), slice(QL * gq, QL * (gq + 1)))

    def vec_of(c):
        return vec_ref[:, QL * tiles[c][1]:QL * (tiles[c][1] + 1)]

    def k_hat_of(c, off):
        return k_ref[at(c, off)] * (1.0 + (a_ref[at(c, off)] - 1.0) * vec_of(c)[1:2])

    def prepare_rows(off, slot):
        k_scaled, sums_in = [], []
        for c in range(G):
            vec = vec_of(c)
            k_scaled.append(k_ref[at(c, off)] * vec[0:1])
            sums_in += [k_scaled[c] * k_scaled[c], r_ref[at(c, off)] * k_hat_of(c, off) * vec[4:5]]
        sums = _seg_sum_mxu(jnp.concatenate(sums_in, axis=0), ones)
        for c in range(G):
            kk = k_scaled[c] / jnp.maximum(jnp.sqrt(sums[2 * c * U:(2 * c + 1) * U]), 1e-12)
            bonus_scr[slot][c] = sums[(2 * c + 1) * U:(2 * c + 2) * U] * v_ref[at(c, off)]
            av_scr[slot][c] = -kk
            bv_scr[slot][c] = kk * a_ref[at(c, off)]
            dv_scr[slot][c] = d_ref[at(c, off)]
            rv_scr[slot][c] = r_ref[at(c, off)]

    def prepare_outer(off, slot, c):
        v = v_ref[at(c, off)]
        k_hat = k_hat_of(c, off)
        v_hi = v.astype(BF16).astype(F32)
        v_lo = v - v_hi
        k_hi = k_hat.astype(BF16).astype(F32)
        k_lo = k_hat - k_hi
        v_stack = jnp.concatenate(head_rows(v_hi) + head_rows(v_hi) + head_rows(v_lo) + [pad_rows], axis=0)
        v_t = v_stack.T
        v_cols = functools.reduce(jnp.add, [v_t[N * h:N * (h + 1)] for h in range(QUAD)]).astype(BF16)
        k_rows = jnp.concatenate(head_rows(k_hi) + head_rows(k_lo) + head_rows(k_hi) + [pad_rows],
                                 axis=0).astype(BF16)
        lhs = jnp.concatenate([jnp.where(lane_step == t, v_cols, jnp.zeros_like(v_cols)) for t in range(U)], axis=0)
        vk_scr[slot][c] = jnp.dot(lhs, k_rows, preferred_element_type=F32)

    def preparation(off, slot):
        return [functools.partial(prepare_rows, off, slot)] + [functools.partial(prepare_outer, off, slot, c)
                                                               for c in range(G)]

    def recur(off, slot, side_work=()):
        side_at = {}
        for n, piece in enumerate(side_work):
            side_at.setdefault((n * U) // len(side_work), []).append(piece)
        S = [S_scr[c] for c in range(G)]
        y_cols = [jnp.zeros((N, QL), F32) for _ in tiles]
        for t in range(U + 1):
            blocks = []
            for c in range(G):
                if t < U:
                    blocks.append((S[c] * av_scr[slot][c, t:t + 1, :]).astype(BF16))
                if t > 0:
                    blocks.append((S[c] * rv_scr[slot][c, t - 1:t, :]).astype(BF16))
            out = jnp.dot(jnp.concatenate(blocks, axis=0), ones, preferred_element_type=F32)
            for piece in side_at.get(t, ()):
                piece()
            per = N * ((t < U) + (t > 0))
            for c in range(G):
                base = c * per
                if t > 0:
                    y_cols[c] = jnp.where(lane_t == t - 1, out[base + per - N:base + per], y_cols[c])
                if t < U:
                    sa = out[base:base + N]
                    S[c] = (S[c] * dv_scr[slot][c, t:t + 1, :] + sa * bv_scr[slot][c, t:t + 1, :]
                            + vk_scr[slot][c, N * t:N * (t + 1), :])
        ys = []
        for c in range(G):
            S_scr[c] = S[c]
            y_t = y_cols[c].T
            ys.append(jnp.concatenate([y_t[N * h:N * h + U, :] for h in range(QUAD)], axis=-1))
        y_all = jnp.concatenate(ys, axis=0)
        yc_all = y_all - _seg_sum_mxu(y_all, ones) * (1.0 / N)
        var_all = _seg_sum_mxu(yc_all * yc_all, ones) * (1.0 / N)
        yn_all = yc_all * lax.rsqrt(var_all + GN_EPS)
        for c in range(G):
            vec = vec_of(c)
            y_ref[at(c, off)] = yn_all[c * U:(c + 1) * U] * vec[2:3] + vec[3:4] + bonus_scr[slot][c]

    n_sub = Tc // U
    for piece in preparation(0, 0):
        piece()
    if n_sub == 1:
        recur(0, 0)
    else:
        def pair(jj, carry):
            off0 = pl.multiple_of(jj * (2 * U), 2 * U)
            off1 = pl.multiple_of(off0 + U, U)
            off2 = pl.multiple_of(jnp.minimum(off0 + 2 * U, Tc - U), U)
            recur(off0, 0, preparation(off1, 1))
            recur(off1, 1, preparation(off2, 0))
            return carry

        lax.fori_loop(0, n_sub // 2, pair, 0)

    @pl.when(ic == pl.num_programs(2) - 1)
    def _():
        for c, (gb, gq) in enumerate(tiles):
            Sc = S_scr[c]
            for h in range(QUAD):
                sT_ref[gb, QUAD * gq + h] = Sc[:, N * h:N * (h + 1)]


def _wkv(r, k, v, d, a, vec8, state, *, Gb, Gq, Tc, U):
    B, T, D = r.shape
    H, N, QL = RWKV_HEADS, RWKV_HEAD, QUAD_LANES
    Tc = min(Tc, T)
    U = min(U, Tc)
    nq = H // QUAD
    assert 3 * QUAD * U <= MXU_DEPTH and U % SUBLANES == 0 and U <= N
    assert T % Tc == 0 and Tc % U == 0 and (Tc == U or (Tc // U) % 2 == 0) and B % Gb == 0 and nq % Gq == 0
    G = Gb * Gq
    seq = pl.BlockSpec((Gb, Tc, QL * Gq), lambda b, p, c: (b, c, p))
    st = pl.BlockSpec((Gb, QUAD * Gq, N, N), lambda b, p, c: (b, p, 0, 0))
    operands = [r, k, v, d, a, vec8] + ([state] if state is not None else [])
    specs = [seq] * 5 + [pl.BlockSpec((8, QL * Gq), lambda b, p, c: (0, p))] + ([st] if state is not None else [])
    n_buf = 1 if Tc == U else 2
    rows = pltpu.VMEM((G, U, QL), F32)
    return pl.pallas_call(
        functools.partial(_wkv_body, Gb, Gq, Tc, U, state is not None),
        grid=(B // Gb, nq // Gq, T // Tc),
        in_specs=specs,
        out_specs=[seq, st],
        out_shape=[jax.ShapeDtypeStruct((B, T, D), F32), jax.ShapeDtypeStruct((B, H, N, N), F32)],
        scratch_shapes=[pltpu.VMEM((G, N, QL), F32)] + ([rows] * 5 + [pltpu.VMEM((G, U * N, QL), F32)]) * n_buf,
        compiler_params=_params(("arbitrary", "arbitrary", "arbitrary")),
        name="wkv",
    )(*operands)


def _rwkv_layer(x, shift, wkv_state, prm, ln_g, ln_b, *, tm, wkv_cfg):
    mu, w_rkv, w1, w2, a1, a2, g1, g2, vec, r_k, w_o = prm
    B, T, D = x.shape
    r, k, v, d, a, g = _rwkv_proj(x, shift, mu, w_rkv, w1, w2, a1, a2, g1, g2, vec, tm=tm)
    vec8 = jnp.concatenate([vec[2:6], r_k.reshape(1, D), jnp.zeros((3, D), F32)], axis=0)
    y, s_new = _wkv(r, k, v, d, a, vec8, wkv_state, **wkv_cfg)
    x_new = _proj_norm(x.reshape(B * T, D), y.reshape(B * T, D), g.reshape(B * T, D), w_o,
                       jnp.zeros((D,), F32), ln_g, ln_b, tm=tm)
    return x_new.reshape(B, T, D), s_new, x[:, -1]


CONV_HIST = CONV_WIDTH - 1
CONV_HALO = 32


def _glu_body(x_ref, w_ref, b_ref, u_ref):
    h = _dot(x_ref[...], w_ref[...]) + b_ref[...]
    D = u_ref.shape[-1]
    u_ref[...] = h[:, :D] * _sigmoid(h[:, D:])


def _glu(x2, w, b, *, tm):
    M, D = x2.shape
    tm = min(tm, M)
    assert M % tm == 0
    return pl.pallas_call(
        _glu_body,
        grid=(M // tm,),
        in_specs=[pl.BlockSpec((tm, D), lambda i: (i, 0)),
                  pl.BlockSpec((D, 2 * D), lambda i: (0, 0)),
                  pl.BlockSpec((1, 2 * D), lambda i: (0, 0))],
        out_specs=pl.BlockSpec((tm, D), lambda i: (i, 0)),
        out_shape=jax.ShapeDtypeStruct((M, D), F32),
        compiler_params=_params(("arbitrary",)),
        name="conf_glu",
    )(x2, w, b.reshape(1, 2 * D))


def _conf_tail(conv, x, bdw_ref, cln_ref, w2_ref, b2_ref, g_ref, b_ref):
    cln = cln_ref[...]
    c = _silu(_layer_norm(conv + bdw_ref[...], cln[0:1], cln[1:2]))
    out = _dot(c, w2_ref[...]) + b2_ref[...]
    return _layer_norm(DN_ALPHA * x + out, g_ref[...], b_ref[...])


def _conf_long_body(seq_len, tm, x_ref, u_ref, halo_ref, wdw_ref, bdw_ref, cln_ref, w2_ref, b2_ref, g_ref, b_ref,
                    y_ref, ext_ref):
    i = pl.program_id(0)
    starts_sequence = (i * tm) % seq_len == 0
    ext_ref[0:CONV_HALO, :] = jnp.where(starts_sequence, 0.0, halo_ref[...])
    ext_ref[CONV_HALO:CONV_HALO + tm, :] = u_ref[...]
    wdw = wdw_ref[...]
    first = CONV_HALO - CONV_HIST
    conv = ext_ref[first:first + tm, :] * wdw[0:1]
    for j in range(1, CONV_WIDTH):
        conv = conv + ext_ref[first + j:first + j + tm, :] * wdw[j:j + 1]
    y_ref[...] = _conf_tail(conv, x_ref[...], bdw_ref, cln_ref, w2_ref, b2_ref, g_ref, b_ref)


def _conf_short_body(seq_len, nb, x_ref, u_ref, hist_ref, wdw_ref, bdw_ref, cln_ref, w2_ref, b2_ref, g_ref, b_ref,
                     y_ref, hist_out_ref, ext_ref):
    D = u_ref.shape[-1]
    ext_ref[:, 0:CONV_HIST, :] = hist_ref[...]
    ext_ref[:, CONV_HIST:CONV_HIST + seq_len, :] = u_ref[...].reshape(nb, seq_len, D)
    wdw = wdw_ref[...]
    conv = ext_ref[:, 0:seq_len, :] * wdw[0:1]
    for j in range(1, CONV_WIDTH):
        conv = conv + ext_ref[:, j:j + seq_len, :] * wdw[j:j + 1]
    hist_out_ref[...] = ext_ref[:, seq_len:seq_len + CONV_HIST, :]
    y_ref[...] = _conf_tail(conv.reshape(nb * seq_len, D), x_ref[...], bdw_ref, cln_ref, w2_ref, b2_ref, g_ref, b_ref)


def _conformer_layer(x, hist, prm, ln_g, ln_b, *, tm):
    w_pw1, b_pw1, w_dw, b_dw, cln, w_pw2, b_pw2 = prm
    B, T, D = x.shape
    M = B * T
    x2 = x.reshape(M, D)
    u = _glu(x2, w_pw1, b_pw1, tm=tm)
    const = lambda r, c: pl.BlockSpec((r, c), lambda i: (0, 0))
    weights = [w_dw, b_dw.reshape(1, D), cln, w_pw2, b_pw2.reshape(1, D), ln_g.reshape(1, D), ln_b.reshape(1, D)]
    wspecs = [const(CONV_WIDTH, D), const(1, D), const(2, D), const(D, D), const(1, D), const(1, D), const(1, D)]
    if hist is None:
        tm = min(tm, T)
        assert T % tm == 0 and tm % CONV_HALO == 0
        row = pl.BlockSpec((tm, D), lambda i: (i, 0))
        halo_spec = pl.BlockSpec((CONV_HALO, D), lambda i: (jnp.maximum(i * (tm // CONV_HALO) - 1, 0), 0))
        y = pl.pallas_call(
            functools.partial(_conf_long_body, T, tm),
            grid=(M // tm,),
            in_specs=[row, row, halo_spec] + wspecs,
            out_specs=row,
            out_shape=jax.ShapeDtypeStruct((M, D), F32),
            scratch_shapes=[pltpu.VMEM((CONV_HALO + tm, D), F32)],
            compiler_params=_params(("arbitrary",)),
            name="conf_conv_long",
        )(x2, u, u, *weights)
        new_hist = u.reshape(B, T, D)[:, T - CONV_HIST:, :]
    else:
        assert T == SUBLANES
        nb = min(tm // T, B)
        assert B % nb == 0
        row = pl.BlockSpec((nb * T, D), lambda i: (i, 0))
        hspec = pl.BlockSpec((nb, CONV_HIST, D), lambda i: (i, 0, 0))
        y, new_hist = pl.pallas_call(
            functools.partial(_conf_short_body, T, nb),
            grid=(B // nb,),
            in_specs=[row, row, hspec] + wspecs,
            out_specs=[row, hspec],
            out_shape=[jax.ShapeDtypeStruct((M, D), F32), jax.ShapeDtypeStruct((B, CONV_HIST, D), F32)],
            scratch_shapes=[pltpu.VMEM((nb, CONV_HIST + T + 2, D), F32)],
            compiler_params=_params(("arbitrary",)),
            name="conf_conv_short",
        )(x2, u, hist, *weights)
    return y.reshape(B, T, D), new_hist


def _rms_norm(z, g):
    return z * lax.rsqrt(jnp.mean(z * z, axis=-1, keepdims=True) + RMS_EPS) * g


def _mla_proj_body(keys_bf16, x_ref, cos_ref, sin_ref, wdq_ref, qn_ref, wqn_ref, wqr_ref, wqs_ref, wckv_ref, wkr_ref,
                   wks_ref, kvn_ref, wuk_ref, qlat_ref, qpe_ref, ckv_ref, kpe_ref, *key_copies):
    x = x_ref[...]
    cos, sin = cos_ref[...], sin_ref[...]
    cq = _rms_norm(_dot(x, wdq_ref[...]), qn_ref[...])
    q_nope = _dot(cq, wqn_ref[...])
    q_rope = _dot(cq, wqr_ref[...])
    q_swap = _dot(cq, wqs_ref[...])
    for h in range(MLA_HEADS):
        q_lat = _dot(q_nope[:, h * QK_NOPE:(h + 1) * QK_NOPE], wuk_ref[h])
        qlat_ref[h] = (q_lat * MLA_SCALE).astype(qlat_ref.dtype)
        sl = slice(h * QK_ROPE, (h + 1) * QK_ROPE)
        qpe_ref[h] = ((q_rope[:, sl] * cos + q_swap[:, sl] * sin) * MLA_SCALE).astype(qpe_ref.dtype)
    ckv = _rms_norm(_dot(x, wckv_ref[...]), kvn_ref[...])
    kpe = _dot(x, wkr_ref[...]) * cos + _dot(x, wks_ref[...]) * sin
    ckv_ref[...] = ckv
    kpe_ref[...] = kpe
    if keys_bf16:
        key_copies[0][...] = ckv.astype(BF16)
        key_copies[1][...] = kpe.astype(BF16)


def _swap_halves(w):
    half = w.shape[-1] // 2
    return jnp.concatenate([w[..., half:], w[..., :half]], axis=-1)


def _rope_tables(pos):
    half = QK_ROPE // 2
    inv = ROPE_THETA ** (-jnp.arange(half, dtype=F32) / half)
    ang = pos[:, None] * inv[None, :]
    cos, sin = jnp.cos(ang), jnp.sin(ang)
    return jnp.concatenate([cos, cos], -1), jnp.concatenate([-sin, sin], -1)


def _mla_proj(x2, pos, w_dq, q_norm, w_uq, w_dkv, kv_norm, w_uk, *, tm, keys_bf16):
    M, D = x2.shape
    H = MLA_HEADS
    tm = min(tm, M)
    assert M % tm == 0
    cos, sin = _rope_tables(pos)
    w_qn = w_uq[:, :, :QK_NOPE].reshape(Q_LORA, H * QK_NOPE)
    w_qr = w_uq[:, :, QK_NOPE:].reshape(Q_LORA, H * QK_ROPE)
    w_qs = _swap_halves(w_uq[:, :, QK_NOPE:]).reshape(Q_LORA, H * QK_ROPE)
    w_ckv, w_kr = w_dkv[:, :KV_LORA], w_dkv[:, KV_LORA:]
    w_ks = _swap_halves(w_kr)
    w_ukT = jnp.transpose(w_uk, (1, 2, 0))
    full = lambda arr: pl.BlockSpec(arr.shape, lambda i: (0,) * arr.ndim)
    row = lambda c: pl.BlockSpec((tm, c), lambda i: (i, 0))
    hrow = lambda c: pl.BlockSpec((H, tm, c), lambda i: (0, i, 0))
    weights = [w_dq, q_norm.reshape(1, Q_LORA), w_qn, w_qr, w_qs, w_ckv, w_kr, w_ks, kv_norm.reshape(1, KV_LORA), w_ukT]
    qdt = BF16 if keys_bf16 else F32
    out_specs = [hrow(KV_LORA), hrow(QK_ROPE), row(KV_LORA), row(QK_ROPE)]
    out_shape = [jax.ShapeDtypeStruct((H, M, KV_LORA), qdt), jax.ShapeDtypeStruct((H, M, QK_ROPE), qdt),
                 jax.ShapeDtypeStruct((M, KV_LORA), F32), jax.ShapeDtypeStruct((M, QK_ROPE), F32)]
    if keys_bf16:
        out_specs += [row(KV_LORA), row(QK_ROPE)]
        out_shape += [jax.ShapeDtypeStruct((M, KV_LORA), BF16), jax.ShapeDtypeStruct((M, QK_ROPE), BF16)]
    return pl.pallas_call(
        functools.partial(_mla_proj_body, keys_bf16),
        grid=(M // tm,),
        in_specs=[row(D), row(QK_ROPE), row(QK_ROPE)] + [full(w) for w in weights],
        out_specs=out_specs,
        out_shape=out_shape,
        compiler_params=_params(("arbitrary",)),
        name="mla_proj",
    )(x2, cos, sin, *weights)


def _scores(ql, qp, ckv, kpe):
    return _dot_t(ql, ckv) + _dot_t(qp, kpe)


def _softmax_step(s, values, m_ref, l_ref, acc_ref):
    m_old = m_ref[...]
    m_new = jnp.maximum(m_old, jnp.max(s, axis=-1, keepdims=True))
    p = jnp.exp(s - m_new)
    alpha = jnp.exp(m_old - m_new)
    l_ref[...] = alpha * l_ref[...] + jnp.sum(p, axis=-1, keepdims=True)
    pv = functools.reduce(jnp.add, [_dot(p[:, ks], v) for ks, v in values])
    acc_ref[...] = alpha * acc_ref[...] + pv
    m_ref[...] = m_new


HEAD_GROUP = 1


def _flash_body(tq, tk, qi_ref, kj_ref, qlat_ref, qpe_ref, ckv_ref, kpe_ref, o_ref, m_ref, l_ref, acc_ref):
    n = pl.program_id(1)
    i, j = qi_ref[n], kj_ref[n]
    H = MLA_HEADS
    last_j = ((i + 1) * tq - 1) // tk

    @pl.when(j == 0)
    def _():
        m_ref[...] = jnp.full(m_ref.shape, -jnp.inf, F32)
        l_ref[...] = jnp.zeros(l_ref.shape, F32)
        acc_ref[...] = jnp.zeros(acc_ref.shape, F32)

    def update(masked):
        ckv, kpe = ckv_ref[...], kpe_ref[...]
        for h0 in range(0, H, HEAD_GROUP):
            hs = slice(h0, h0 + HEAD_GROUP)
            rows = slice(h0 * tq, (h0 + HEAD_GROUP) * tq)
            s = _scores(qlat_ref[hs].reshape(HEAD_GROUP * tq, KV_LORA), qpe_ref[hs].reshape(HEAD_GROUP * tq, QK_ROPE),
                        ckv, kpe)
            if masked:
                q_pos = i * tq + lax.broadcasted_iota(jnp.int32, (HEAD_GROUP * tq, 1), 0) % tq
                k_pos = j * tk + lax.broadcasted_iota(jnp.int32, (1, tk), 1)
                s = jnp.where(k_pos <= q_pos, s, -jnp.inf)
            _softmax_step(s, [(slice(None), ckv)], m_ref.at[rows], l_ref.at[rows], acc_ref.at[rows])

    @pl.when(j < last_j)
    def _():
        update(False)

    @pl.when(j == last_j)
    def _():
        update(True)
        o_ref[...] = (acc_ref[...] / l_ref[...]).reshape(H, tq, KV_LORA)


def _flash_attention(q_lat, q_pe, ckv, kpe, B, T, *, tq, tk):
    H = MLA_HEADS
    tq, tk = min(tq, T), min(tk, T)
    assert T % tq == 0 and T % tk == 0 and tk % tq == 0
    nq, nk = T // tq, T // tk
    pairs = [(i, j) for i in range(nq) for j in range(((i + 1) * tq - 1) // tk + 1)]
    qi = jnp.asarray([p[0] for p in pairs], jnp.int32)
    kj = jnp.asarray([p[1] for p in pairs], jnp.int32)
    qmap = lambda b, n, qi, kj: (0, b * nq + qi[n], 0)
    kmap = lambda b, n, qi, kj: (b * nk + kj[n], 0)
    grid_spec = pltpu.PrefetchScalarGridSpec(
        num_scalar_prefetch=2,
        grid=(B, len(pairs)),
        in_specs=[pl.BlockSpec((H, tq, KV_LORA), qmap), pl.BlockSpec((H, tq, QK_ROPE), qmap),
                  pl.BlockSpec((tk, KV_LORA), kmap), pl.BlockSpec((tk, QK_ROPE), kmap)],
        out_specs=pl.BlockSpec((H, tq, KV_LORA), qmap),
        scratch_shapes=[pltpu.VMEM((H * tq, 1), F32), pltpu.VMEM((H * tq, 1), F32), pltpu.VMEM((H * tq, KV_LORA), F32)],
    )
    return pl.pallas_call(
        functools.partial(_flash_body, tq, tk),
        grid_spec=grid_spec,
        out_shape=jax.ShapeDtypeStruct((H, B * T, KV_LORA), F32),
        compiler_params=_params(("arbitrary", "arbitrary")),
        name="mla_flash",
    )(qi, kj, q_lat, q_pe, ckv, kpe)


def _paged_body(PG, T, page, pt_ref, qlat_ref, qpe_ref, ckvn_ref, kpen_ref, *rest):
    pages_ckv, pages_kpe = rest[:PG], rest[PG:2 * PG]
    o_ref, m_ref, l_ref, acc_ref, selfk_ref, selfp_ref = rest[2 * PG:]
    g = pl.program_id(1)
    H = MLA_HEADS
    ql = qlat_ref[...].reshape(H * T, KV_LORA).astype(BF16)
    qp = qpe_ref[...].reshape(H * T, QK_ROPE).astype(BF16)

    @pl.when(g == 0)
    def _():
        m_ref[...] = jnp.full(m_ref.shape, -jnp.inf, F32)
        l_ref[...] = jnp.zeros(l_ref.shape, F32)
        acc_ref[...] = jnp.zeros(acc_ref.shape, F32)

    keys = jnp.concatenate([pages_ckv[p][0, 0].astype(BF16) for p in range(PG)], axis=0)
    rope_keys_t = jnp.concatenate([pages_kpe[p][0, 0].astype(BF16) for p in range(PG)], axis=1)
    _softmax_step(_dot_t(ql, keys) + _dot(qp, rope_keys_t), [(slice(None), keys)], m_ref, l_ref, acc_ref)

    @pl.when(g == pl.num_programs(1) - 1)
    def _():
        selfk_ref[...] = jnp.zeros(selfk_ref.shape, F32)
        selfp_ref[...] = jnp.zeros(selfp_ref.shape, F32)
        selfk_ref[0:T, :] = ckvn_ref[...]
        selfp_ref[0:T, :] = kpen_ref[...]
        ckv = selfk_ref[...]
        s = _scores(ql, qp, ckv, selfp_ref[...])
        q_pos = lax.broadcasted_iota(jnp.int32, (H * T, 1), 0) % T
        k_pos = lax.broadcasted_iota(jnp.int32, (1, page), 1)
        s = jnp.where(k_pos <= q_pos, s, -jnp.inf)
        _softmax_step(s, [(slice(None), ckv)], m_ref, l_ref, acc_ref)
        o_ref[...] = (acc_ref[...] / l_ref[...]).reshape(H, T, KV_LORA)


def _paged_attention(q_lat, q_pe, ckv, kpe, cache_ckv, cache_kpe_t, page_table, layer, T, *, PG):
    H = MLA_HEADS
    DB, n_pages = page_table.shape
    page = cache_ckv.shape[2]
    assert n_pages % PG == 0 and T <= page
    qmap = lambda b, g, pt: (0, b, 0)
    nmap = lambda b, g, pt: (b, 0)
    pmap = lambda p: (lambda b, g, pt: (layer, pt[b * n_pages + g * PG + p], 0, 0))
    grid_spec = pltpu.PrefetchScalarGridSpec(
        num_scalar_prefetch=1,
        grid=(DB, n_pages // PG),
        in_specs=[pl.BlockSpec((H, T, KV_LORA), qmap), pl.BlockSpec((H, T, QK_ROPE), qmap),
                  pl.BlockSpec((T, KV_LORA), nmap), pl.BlockSpec((T, QK_ROPE), nmap)]
                 + [pl.BlockSpec((1, 1, page, KV_LORA), pmap(p)) for p in range(PG)]
                 + [pl.BlockSpec((1, 1, QK_ROPE, page), pmap(p)) for p in range(PG)],
        out_specs=pl.BlockSpec((H, T, KV_LORA), qmap),
        scratch_shapes=[pltpu.VMEM((H * T, 1), F32), pltpu.VMEM((H * T, 1), F32), pltpu.VMEM((H * T, KV_LORA), F32),
                        pltpu.VMEM((page, KV_LORA), F32), pltpu.VMEM((page, QK_ROPE), F32)],
    )
    return pl.pallas_call(
        functools.partial(_paged_body, PG, T, page),
        grid_spec=grid_spec,
        out_shape=jax.ShapeDtypeStruct((H, DB * T, KV_LORA), F32),
        compiler_params=_params(("arbitrary", "arbitrary")),
        name="mla_paged",
    )(page_table.reshape(-1), q_lat, q_pe, ckv, kpe, *([cache_ckv] * PG), *([cache_kpe_t] * PG))


def _mla_out_body(x_ref, o_ref, wuv_ref, wo_ref, g_ref, b_ref, y_ref):
    heads = [_dot(o_ref[h], wuv_ref[h]) for h in range(MLA_HEADS)]
    out = _dot(jnp.concatenate(heads, axis=-1), wo_ref[...])
    y_ref[...] = _layer_norm(DN_ALPHA * x_ref[...] + out, g_ref[...], b_ref[...])


def _mla_out(x2, o_lat, w_uv, w_o, ln_g, ln_b, *, tm):
    M, D = x2.shape
    H = MLA_HEADS
    tm = min(tm, M)
    assert M % tm == 0
    w_uvh = jnp.transpose(w_uv, (1, 0, 2))
    const = lambda shape: pl.BlockSpec(shape, lambda i: (0,) * len(shape))
    return pl.pallas_call(
        _mla_out_body,
        grid=(M // tm,),
        in_specs=[pl.BlockSpec((tm, D), lambda i: (i, 0)), pl.BlockSpec((H, tm, KV_LORA), lambda i: (0, i, 0)),
                  const(w_uvh.shape), const(w_o.shape), const((1, D)), const((1, D))],
        out_specs=pl.BlockSpec((tm, D), lambda i: (i, 0)),
        out_shape=jax.ShapeDtypeStruct((M, D), F32),
        compiler_params=_params(("arbitrary",)),
        name="mla_out",
    )(x2, o_lat, w_uvh, w_o, ln_g.reshape(1, D), ln_b.reshape(1, D))


def _mla_layer(x, pos0, cache, prm, ln_g, ln_b, *, tm, attn_cfg):
    w_dq, q_norm, w_uq, w_dkv, kv_norm, w_uk, w_uv, w_o = prm
    B, T, D = x.shape
    x2 = x.reshape(B * T, D)
    pos = jnp.tile(pos0 + jnp.arange(T, dtype=F32), B)
    proj = _mla_proj(x2, pos, w_dq, q_norm, w_uq, w_dkv, kv_norm, w_uk, tm=tm, keys_bf16=cache is None)
    q_lat, q_pe, ckv, kpe = proj[:4]
    if cache is None:
        o_lat = _flash_attention(q_lat, q_pe, proj[4], proj[5], B, T, **attn_cfg)
    else:
        o_lat = _paged_attention(q_lat, q_pe, ckv, kpe, *cache, T, **attn_cfg)
    y = _mla_out(x2, o_lat, w_uv, w_o, ln_g, ln_b, tm=tm)
    return y.reshape(B, T, D), ckv.reshape(B, T, KV_LORA), kpe.reshape(B, T, QK_ROPE)


def kernel(x_prompt, x_sample, state_wkv, state_shift, state_conv, cache_mla_ckv, cache_mla_kpe, state_ffn_conv, page_table, ln_g, ln_b, rwkv_mu, rwkv_w_rkv, rwkv_w1, rwkv_w2, rwkv_a1, rwkv_a2, rwkv_g1, rwkv_g2, rwkv_vec, rwkv_r_k, rwkv_w_o, conf_w_pw1, conf_b_pw1, conf_w_dw, conf_b_dw, conf_ln, conf_w_pw2, conf_b_pw2, mla_w_dq, mla_q_norm, mla_w_uq, mla_w_dkv, mla_kv_norm, mla_w_uk, mla_w_uv, mla_w_o, ffn_w_in, ffn_w_dw, ffn_b_dw, ffn_w_out):
    bf = lambda w: w.astype(BF16)
    past_len = page_table.shape[1] * cache_mla_ckv.shape[2]
    xp, xs = x_prompt, x_sample
    out_p = dict(wkv=[], shift=[], conv=[], ckv=[], kpe=[], ffn=[])
    out_s = dict(wkv=[], shift=[], conv=[], ckv=[], kpe=[], ffn=[])
    for i in range(DEPTH):
        j, kind = i // N_MIXERS, i % N_MIXERS
        g0, b0, g1, b1 = ln_g[i, 0], ln_b[i, 0], ln_g[i, 1], ln_b[i, 1]
        if kind == 0:
            prm = (rwkv_mu[j], bf(rwkv_w_rkv[j]), bf(rwkv_w1[j]), bf(rwkv_w2[j]), bf(rwkv_a1[j]), bf(rwkv_a2[j]),
                   bf(rwkv_g1[j]), bf(rwkv_g2[j]), rwkv_vec[j], rwkv_r_k[j], bf(rwkv_w_o[j]))
            xp, st, sh = _rwkv_layer(xp, None, None, prm, g0, b0, tm=512, wkv_cfg=dict(Gb=2, Gq=4, Tc=128, U=16))
            out_p["wkv"].append(st)
            out_p["shift"].append(sh)
            xs, st, sh = _rwkv_layer(xs, state_shift[j], state_wkv[j], prm, g0, b0, tm=512,
                                     wkv_cfg=dict(Gb=2, Gq=4, Tc=8, U=8))
            out_s["wkv"].append(st)
            out_s["shift"].append(sh)
        elif kind == 1:
            prm = (bf(conf_w_pw1[j]), conf_b_pw1[j], conf_w_dw[j], conf_b_dw[j], conf_ln[j], bf(conf_w_pw2[j]),
                   conf_b_pw2[j])
            xp, cb = _conformer_layer(xp, None, prm, g0, b0, tm=256)
            out_p["conv"].append(cb)
            xs, cb = _conformer_layer(xs, state_conv[j], prm, g0, b0, tm=256)
            out_s["conv"].append(cb)
        else:
            prm = (bf(mla_w_dq[j]), mla_q_norm[j], bf(mla_w_uq[j]), bf(mla_w_dkv[j]), mla_kv_norm[j],
                   bf(mla_w_uk[j]), bf(mla_w_uv[j]), bf(mla_w_o[j]))
            xp, ckv, kpe = _mla_layer(xp, 0.0, None, prm, g0, b0, tm=512, attn_cfg=dict(tq=256, tk=1024))
            out_p["ckv"].append(ckv)
            out_p["kpe"].append(kpe)
            xs, ckv, kpe = _mla_layer(xs, float(past_len), (cache_mla_ckv, jnp.swapaxes(cache_mla_kpe, 2, 3), page_table, j), prm,
                                      g0, b0, tm=512, attn_cfg=dict(PG=16))
            out_s["ckv"].append(ckv)
            out_s["kpe"].append(kpe)
        w_in, w_out = bf(ffn_w_in[i]), bf(ffn_w_out[i])
        xp, fb = _conv_ffn(xp, None, w_in, ffn_w_dw[i], ffn_b_dw[i], w_out, g1, b1, tm=512, fc=1408)
        out_p["ffn"].append(fb)
        xs, fb = _conv_ffn(xs, state_ffn_conv[i], w_in, ffn_w_dw[i], ffn_b_dw[i], w_out, g1, b1, tm=512, fc=1408)
        out_s["ffn"].append(fb)
    names = ("wkv", "shift", "conv", "ckv", "kpe", "ffn")
    return (xp, xs) + tuple(jnp.stack(out_p[n]) for n in names) + tuple(jnp.stack(out_s[n]) for n in names)
```

```python
import functools

import jax
import jax.numpy as jnp
from jax import lax
from jax.experimental import pallas as pl
from jax.experimental.pallas import tpu as pltpu

D_MODEL = 1024
DEPTH = 4
N_MIXERS = 3
DN_ALPHA = (2 * DEPTH) ** 0.25
LN_EPS = 1e-5
RWKV_HEAD = 64
RWKV_HEADS = D_MODEL // RWKV_HEAD
GN_EPS = 64e-5
CONV_WIDTH = 31
MLA_HEADS = 8
QK_NOPE = 128
QK_ROPE = 64
V_HEAD = 128
KV_LORA = 256
Q_LORA = 384
ROPE_THETA = 10000.0
MLA_SCALE = (QK_NOPE + QK_ROPE) ** -0.5
RMS_EPS = 1e-6
D_FF = 2816
FFN_CONV_WIDTH = 3

SUBLANES = 8
LANES = 128
VMEM_LIMIT = 56 * 1024 * 1024

BF16 = jnp.bfloat16
F32 = jnp.float32


def _params(sem):
    return pltpu.CompilerParams(dimension_semantics=sem, vmem_limit_bytes=VMEM_LIMIT)


def _dot(a, b):
    return jnp.dot(a.astype(BF16), b.astype(BF16), preferred_element_type=F32)


def _dot_t(a, b):
    return lax.dot_general(a.astype(BF16), b.astype(BF16), (((1,), (1,)), ((), ())),
                           preferred_element_type=F32)


def _layer_norm(z, g, b, eps=LN_EPS):
    mu = jnp.mean(z, axis=-1, keepdims=True)
    zc = z - mu
    var = jnp.mean(zc * zc, axis=-1, keepdims=True)
    return zc * lax.rsqrt(var + eps) * g + b


def _sigmoid(z):
    return 1.0 / (1.0 + jnp.exp(-z))


def _silu(z):
    return z * _sigmoid(z)


def _prev_rows(cur, shift, pos, fill):
    out = pltpu.roll(cur, shift, 0)
    for p in range(shift):
        out = jnp.where(pos == p, fill[shift - 1 - p], out)
    return out


def _ffn_body(seq_len, tm, x_ref, halo_ref, wa_ref, wb_ref, wdw_ref, bdw_ref, wo_ref, g_ref, b_ref,
              y_ref, tail_ref, acc_ref, xb_ref):
    i, f = pl.program_id(0), pl.program_id(1)

    @pl.when(f == 0)
    def _():
        xb_ref[...] = x_ref[...].astype(BF16)

    xb = xb_ref[...]
    a = _dot(xb, wa_ref[...])
    b = _dot(xb, wb_ref[...])
    fc = a.shape[-1]
    rows = lax.broadcasted_iota(jnp.int32, (tm, 1), 0)
    if seq_len >= tm:
        starts_sequence = (i * tm) % seq_len == 0
        ah = _dot(halo_ref[...], wa_ref[...])
        ah = jnp.where(starts_sequence, 0.0, ah)
        pos = rows
        before1, before2 = ah[SUBLANES - 1:SUBLANES], ah[SUBLANES - 2:SUBLANES - 1]
    else:
        hist = halo_ref[...]
        nseq = tm // seq_len
        expand = lambda r: jnp.broadcast_to(r, (nseq, seq_len, fc)).reshape(tm, fc)
        pos = rows % seq_len
        before1, before2 = expand(hist[:, 1:2, :]), expand(hist[:, 0:1, :])
    a1 = _prev_rows(a, 1, pos, [before1])
    a2 = _prev_rows(a, 2, pos, [before1, before2])
    wdw = wdw_ref[...]
    c = a2 * wdw[0:1] + a1 * wdw[1:2] + a * wdw[2:3] + bdw_ref[...]
    part = _dot(_silu(c) * b, wo_ref[...])

    @pl.when(f == 0)
    def _():
        acc_ref[...] = part

    @pl.when(f > 0)
    def _():
        acc_ref[...] += part

    if seq_len >= tm:
        tail_ref[0] = a[tm - SUBLANES:tm]
    else:
        tail_ref[...] = a.reshape(tm // seq_len, seq_len, fc)

    @pl.when(f == pl.num_programs(1) - 1)
    def _():
        y_ref[...] = _layer_norm(DN_ALPHA * x_ref[...] + acc_ref[...], g_ref[...], b_ref[...])


def _conv_ffn(x, hist, w_in, w_dw, b_dw, w_out, ln_g, ln_b, *, tm, fc):
    B, T, D = x.shape
    F = w_out.shape[0]
    M = B * T
    x2 = x.reshape(M, D)
    nF = F // fc
    long_seq = hist is None
    if long_seq:
        tm = min(tm, T)
        assert T % tm == 0 and tm % SUBLANES == 0
        halo = x2
        halo_spec = pl.BlockSpec((SUBLANES, D), lambda i, f: (jnp.maximum(i * (tm // SUBLANES) - 1, 0), 0))
        tail_shape = jax.ShapeDtypeStruct((M // tm, SUBLANES, F), F32)
        tail_spec = pl.BlockSpec((1, SUBLANES, fc), lambda i, f: (i, 0, f))
    else:
        assert T == SUBLANES
        tm = min(tm, M)
        assert M % tm == 0 and tm % T == 0
        halo = hist
        halo_spec = pl.BlockSpec((tm // T, FFN_CONV_WIDTH - 1, fc), lambda i, f: (i, 0, f))
        tail_shape = jax.ShapeDtypeStruct((B, T, F), F32)
        tail_spec = pl.BlockSpec((tm // T, T, fc), lambda i, f: (i, 0, f))
    y, tail = pl.pallas_call(
        functools.partial(_ffn_body, T, tm),
        grid=(M // tm, nF),
        in_specs=[
            pl.BlockSpec((tm, D), lambda i, f: (i, 0)),
            halo_spec,
            pl.BlockSpec((D, fc), lambda i, f: (0, f)),
            pl.BlockSpec((D, fc), lambda i, f: (0, nF + f)),
            pl.BlockSpec((FFN_CONV_WIDTH, fc), lambda i, f: (0, f)),
            pl.BlockSpec((1, fc), lambda i, f: (0, f)),
            pl.BlockSpec((fc, D), lambda i, f: (f, 0)),
            pl.BlockSpec((1, D), lambda i, f: (0, 0)),
            pl.BlockSpec((1, D), lambda i, f: (0, 0)),
        ],
        out_specs=[pl.BlockSpec((tm, D), lambda i, f: (i, 0)), tail_spec],
        out_shape=[jax.ShapeDtypeStruct((M, D), F32), tail_shape],
        scratch_shapes=[pltpu.VMEM((tm, D), F32), pltpu.VMEM((tm, D), BF16)],
        compiler_params=_params(("arbitrary", "arbitrary")),
        name="conv_ffn",
    )(x2, halo, w_in, w_in, w_dw, b_dw.reshape(1, F), w_out, ln_g.reshape(1, D), ln_b.reshape(1, D))
    if long_seq:
        tail = tail[T // tm - 1::T // tm]
    return y.reshape(B, T, D), tail[:, -(FFN_CONV_WIDTH - 1):, :]


def _proj_norm_body(gated, *refs):
    if gated:
        x_ref, h_ref, gate_ref, w_ref, bias_ref, g_ref, b_ref, y_ref = refs
        h = h_ref[...] * gate_ref[...]
    else:
        x_ref, h_ref, w_ref, bias_ref, g_ref, b_ref, y_ref = refs
        h = h_ref[...]
    out = _dot(h, w_ref[...]) + bias_ref[...]
    y_ref[...] = _layer_norm(DN_ALPHA * x_ref[...] + out, g_ref[...], b_ref[...])


def _proj_norm(x, h, gate, w, bias, ln_g, ln_b, *, tm):
    M, D = x.shape
    K = h.shape[1]
    tm = min(tm, M)
    assert M % tm == 0
    row = lambda c: pl.BlockSpec((tm, c), lambda i: (i, 0))
    const = lambda r, c: pl.BlockSpec((r, c), lambda i: (0, 0))
    operands = [x, h] + ([gate] if gate is not None else []) + [w, bias.reshape(1, D), ln_g.reshape(1, D), ln_b.reshape(1, D)]
    specs = [row(D), row(K)] + ([row(K)] if gate is not None else []) + [const(K, D), const(1, D), const(1, D), const(1, D)]
    return pl.pallas_call(
        functools.partial(_proj_norm_body, gate is not None),
        grid=(M // tm,),
        in_specs=specs,
        out_specs=row(D),
        out_shape=jax.ShapeDtypeStruct((M, D), F32),
        compiler_params=_params(("arbitrary",)),
        name="proj_norm",
    )(*operands)


def _softplus(z):
    return jnp.maximum(z, 0.0) + jnp.log(1.0 + jnp.exp(-jnp.abs(z)))


def _rwkv_proj_body(seq_len, tm, x_ref, halo_ref, mu_ref, wrkv_ref, w1_ref, w2_ref, a1_ref, a2_ref,
                    g1_ref, g2_ref, vec_ref, r_ref, k_ref, v_ref, d_ref, a_ref, g_ref):
    i = pl.program_id(0)
    x = x_ref[...]
    D = x.shape[-1]
    rows = lax.broadcasted_iota(jnp.int32, (tm, 1), 0)
    if seq_len >= tm:
        starts_sequence = (i * tm) % seq_len == 0
        before = jnp.where(starts_sequence, 0.0, halo_ref[SUBLANES - 1:SUBLANES, :])
        pos = rows
    else:
        nseq = tm // seq_len
        before = jnp.broadcast_to(halo_ref[...], (nseq, seq_len, D)).reshape(tm, D)
        pos = rows % seq_len
    dx = _prev_rows(x, 1, pos, [before]) - x
    mu = mu_ref[...]
    mix = lambda j: x + dx * mu[j:j + 1]
    vec = vec_ref[...]
    r_ref[...] = _dot(mix(0), wrkv_ref[0])
    k_ref[...] = _dot(mix(1), wrkv_ref[1])
    v_ref[...] = _dot(mix(2), wrkv_ref[2])
    w_pre = vec[0:1] + _dot(jnp.tanh(_dot(mix(3), w1_ref[...])), w2_ref[...])
    d_ref[...] = jnp.exp(-jnp.exp(-_softplus(-w_pre) - 0.5))
    a_ref[...] = _sigmoid(vec[1:2] + _dot(_dot(mix(4), a1_ref[...]), a2_ref[...]))
    g_ref[...] = _dot(_sigmoid(_dot(mix(5), g1_ref[...])), g2_ref[...])


def _rwkv_proj(x, shift, mu, w_rkv, w1, w2, a1, a2, g1, g2, vec, *, tm):
    B, T, D = x.shape
    M = B * T
    x2 = x.reshape(M, D)
    if shift is None:
        tm = min(tm, T)
        assert T % tm == 0
        halo = x2
        halo_spec = pl.BlockSpec((SUBLANES, D), lambda i: (jnp.maximum(i * (tm // SUBLANES) - 1, 0), 0))
    else:
        assert T == SUBLANES
        tm = min(tm, M)
        assert M % tm == 0
        halo = shift.reshape(B, 1, D)
        halo_spec = pl.BlockSpec((tm // T, 1, D), lambda i: (i, 0, 0))
    full = lambda arr: pl.BlockSpec(arr.shape, lambda i: (0,) * arr.ndim)
    row = pl.BlockSpec((tm, D), lambda i: (i, 0))
    outs = pl.pallas_call(
        functools.partial(_rwkv_proj_body, T, tm),
        grid=(M // tm,),
        in_specs=[row, halo_spec] + [full(w) for w in (mu, w_rkv, w1, w2, a1, a2, g1, g2, vec)],
        out_specs=[row] * 6,
        out_shape=[jax.ShapeDtypeStruct((M, D), F32)] * 6,
        compiler_params=_params(("arbitrary",)),
        name="rwkv_proj",
    )(x2, halo, mu, w_rkv, w1, w2, a1, a2, g1, g2, vec)
    return [o.reshape(B, T, D) for o in outs]


QUAD = 4
QUAD_LANES = QUAD * RWKV_HEAD
HALF_HEAD = RWKV_HEAD // 2
MXU_DEPTH = 256


def _fold_lanes(w):
    lead = w.shape[:-1]
    w = w.reshape(lead + (RWKV_HEADS // QUAD, QUAD, 2, HALF_HEAD))
    return jnp.swapaxes(w, -3, -2).reshape(lead + (RWKV_HEADS * RWKV_HEAD,))


def _split3(z):
    z1 = z.astype(BF16)
    r1 = z - z1.astype(F32)
    z2 = r1.astype(BF16)
    z3 = (r1 - z2.astype(F32)).astype(BF16)
    return z1, z2, z3


def _seg_sum_mxu(z, ones):
    n = z.shape[0]
    s = jnp.dot(jnp.concatenate(_split3(z), axis=0), ones, preferred_element_type=F32)
    return s[0:n] + s[n:2 * n] + s[2 * n:3 * n]


def _wkv_body(Gb, Gq, Tc, U, has_state, r_ref, k_ref, v_ref, d_ref, a_ref, vec_ref, *rest):
    if has_state:
        s0_ref, y_ref, sT_ref, S_scr, *bufs = rest
    else:
        y_ref, sT_ref, S_scr, *bufs = rest
    N, QL, HL = RWKV_HEAD, QUAD_LANES, LANES
    av_scr, bv_scr, dv_scr, rv_scr, bonus_scr, vk_scr = [bufs[i::6] for i in range(6)]
    ic = pl.program_id(2)
    tiles = [(gb, gq) for gb in range(Gb) for gq in range(Gq)]
    G = len(tiles)
    n_parts = 3 * QUAD * U
    KP = LANES if n_parts <= LANES else MXU_DEPTH
    iota = lambda shape, axis: lax.broadcasted_iota(jnp.int32, shape, axis)
    head_folded = lambda lane: (lane % HL) // HALF_HEAD
    head_plain = lambda lane: lane // N
    lane_slot = iota((N, HL), 1) % HALF_HEAD
    lane_step = iota((N, KP), 1) % U
    ri, ci = iota((QL, QL), 0), iota((QL, QL), 1)
    block_ones = lambda same: jnp.where(same, 1.0, 0.0).astype(BF16)
    ones_ff = block_ones(head_folded(ri) == head_folded(ci))
    ones_fp = block_ones(head_folded(ri) == head_plain(ci))
    ones_pp = block_ones(head_plain(ri) == head_plain(ci))
    ones_step = block_ones((ri // HL == ci // HL) & (head_folded(ri) == head_folded(ci)))
    pad_rows = jnp.zeros((KP - n_parts, QL), F32)
    no_rows = jnp.zeros((N, HL), BF16)

    @pl.when(ic == 0)
    def _():
        for c, (gb, gq) in enumerate(tiles):
            if has_state:
                S_scr[c] = jnp.concatenate([s0_ref[gb, QUAD * gq + h][:, HALF_HEAD * half:HALF_HEAD * (half + 1)]
                                            for half in range(2) for h in range(QUAD)], axis=-1)
            else:
                S_scr[c] = jnp.zeros((N, QL), F32)

    def head_rows(z, head_of_lane):
        heads = head_of_lane(iota((U, QL), 1))
        return [jnp.where(heads == h, z, 0.0) for h in range(QUAD)]

    def half_sum(z):
        return (z[:, 0:HL] + z[:, HL:QL]).astype(BF16)

    def at(c, off):
        gb, gq = tiles[c]
        return (gb, pl.ds(off, U), slice(QL * gq, QL * (gq + 1)))

    def vec_of(c):
        return vec_ref[:, QL * tiles[c][1]:QL * (tiles[c][1] + 1)]

    def k_hat_of(c, off):
        return k_ref[at(c, off)] * (1.0 + (a_ref[at(c, off)] - 1.0) * vec_of(c)[1:2])

    def prepare_rows(off, slot):
        k_scaled, bonus_in = [], []
        for c in range(G):
            vec = vec_of(c)
            k_scaled.append(k_ref[at(c, off)] * vec[0:1])
            bonus_in.append(r_ref[at(c, off)] * k_hat_of(c, off) * vec[4:5])
        norms = _seg_sum_mxu(jnp.concatenate([z * z for z in k_scaled], axis=0), ones_ff)
        bonus = _seg_sum_mxu(jnp.concatenate(bonus_in, axis=0), ones_fp)
        for c in range(G):
            kk = k_scaled[c] / jnp.maximum(jnp.sqrt(norms[c * U:(c + 1) * U]), 1e-12)
            bonus_scr[slot][c] = bonus[c * U:(c + 1) * U] * v_ref[at(c, off)]
            av_scr[slot][c] = -kk
            bv_scr[slot][c] = kk * a_ref[at(c, off)]
            dv_scr[slot][c] = d_ref[at(c, off)]
            rv_scr[slot][c] = r_ref[at(c, off)]

    def prepare_outer(off, slot, c):
        v = v_ref[at(c, off)]
        k_hat = k_hat_of(c, off)
        v_hi = v.astype(BF16).astype(F32)
        v_lo = v - v_hi
        k_hi = k_hat.astype(BF16).astype(F32)
        k_lo = k_hat - k_hi
        v_stack = jnp.concatenate(head_rows(v_hi, head_plain) + head_rows(v_hi, head_plain)
                                  + head_rows(v_lo, head_plain) + [pad_rows], axis=0)
        v_t = v_stack.T
        v_cols = functools.reduce(jnp.add, [v_t[N * h:N * (h + 1)] for h in range(QUAD)]).astype(BF16)
        k_rows = jnp.concatenate(head_rows(k_hi, head_folded) + head_rows(k_lo, head_folded)
                                 + head_rows(k_hi, head_folded) + [pad_rows], axis=0).astype(BF16)
        lhs = jnp.concatenate([jnp.where(lane_step == t, v_cols, jnp.zeros_like(v_cols)) for t in range(U)], axis=0)
        vk_scr[slot][c] = jnp.dot(lhs, k_rows, preferred_element_type=F32)

    def preparation(off, slot):
        return [functools.partial(prepare_rows, off, slot)] + [functools.partial(prepare_outer, off, slot, c)
                                                               for c in range(G)]

    def recur(off, slot, side_work=()):
        side_at = {}
        for n, piece in enumerate(side_work):
            side_at.setdefault((n * U) // len(side_work), []).append(piece)
        S = [S_scr[c] for c in range(G)]
        y_cols = [jnp.zeros((N, HL), F32) for _ in tiles]
        for t in range(U + 1):
            blocks = []
            for c in range(G):
                sa_in = half_sum(S[c] * av_scr[slot][c, t:t + 1, :]) if t < U else no_rows
                y_in = half_sum(S[c] * rv_scr[slot][c, t - 1:t, :]) if t > 0 else no_rows
                blocks.append(jnp.concatenate([sa_in, y_in], axis=1))
            out = jnp.dot(jnp.concatenate(blocks, axis=0), ones_step, preferred_element_type=F32)
            for piece in side_at.get(t, ()):
                piece()
            for c in range(G):
                o = out[N * c:N * (c + 1)]
                if t > 0:
                    y_cols[c] = jnp.where(lane_slot == t - 1, o[:, HL:QL], y_cols[c])
                if t < U:
                    sa = jnp.concatenate([o[:, 0:HL], o[:, 0:HL]], axis=1)
                    S[c] = (S[c] * dv_scr[slot][c, t:t + 1, :] + sa * bv_scr[slot][c, t:t + 1, :]
                            + vk_scr[slot][c, N * t:N * (t + 1), :])
        ys = []
        for c in range(G):
            S_scr[c] = S[c]
            y_t = y_cols[c].T
            ys.append(jnp.concatenate([y_t[HALF_HEAD * h:HALF_HEAD * h + U, :] for h in range(QUAD)], axis=-1))
        y_all = jnp.concatenate(ys, axis=0)
        yc_all = y_all - _seg_sum_mxu(y_all, ones_pp) * (1.0 / N)
        var_all = _seg_sum_mxu(yc_all * yc_all, ones_pp) * (1.0 / N)
        yn_all = yc_all * lax.rsqrt(var_all + GN_EPS)
        for c in range(G):
            vec = vec_of(c)
            y_ref[at(c, off)] = yn_all[c * U:(c + 1) * U] * vec[2:3] + vec[3:4] + bonus_scr[slot][c]

    n_sub = Tc // U
    for piece in preparation(0, 0):
        piece()
    if n_sub == 1:
        recur(0, 0)
    else:
        def pair(jj, carry):
            off0 = pl.multiple_of(jj * (2 * U), 2 * U)
            off1 = pl.multiple_of(off0 + U, U)
            off2 = pl.multiple_of(jnp.minimum(off0 + 2 * U, Tc - U), U)
            recur(off0, 0, preparation(off1, 1))
            recur(off1, 1, preparation(off2, 0))
            return carry

        lax.fori_loop(0, n_sub // 2, pair, 0)

    @pl.when(ic == pl.num_programs(2) - 1)
    def _():
        for c, (gb, gq) in enumerate(tiles):
            Sc = S_scr[c]
            for h in range(QUAD):
                sT_ref[gb, QUAD * gq + h] = jnp.concatenate(
                    [Sc[:, HALF_HEAD * h:HALF_HEAD * (h + 1)], Sc[:, HL + HALF_HEAD * h:HL + HALF_HEAD * (h + 1)]], axis=1)


def _wkv(r, k, v, d, a, vec8, state, *, Gb, Gq, Tc, U):
    B, T, D = r.shape
    H, N, QL = RWKV_HEADS, RWKV_HEAD, QUAD_LANES
    Tc = min(Tc, T)
    U = min(U, Tc)
    nq = H // QUAD
    assert 3 * QUAD * U <= MXU_DEPTH and U % SUBLANES == 0 and U <= HALF_HEAD
    assert T % Tc == 0 and Tc % U == 0 and (Tc == U or (Tc // U) % 2 == 0) and B % Gb == 0 and nq % Gq == 0
    G = Gb * Gq
    seq = pl.BlockSpec((Gb, Tc, QL * Gq), lambda b, p, c: (b, c, p))
    st = pl.BlockSpec((Gb, QUAD * Gq, N, N), lambda b, p, c: (b, p, 0, 0))
    operands = [r, k, v, d, a, vec8] + ([state] if state is not None else [])
    specs = [seq] * 5 + [pl.BlockSpec((8, QL * Gq), lambda b, p, c: (0, p))] + ([st] if state is not None else [])
    n_buf = 1 if Tc == U else 2
    rows = pltpu.VMEM((G, U, QL), F32)
    return pl.pallas_call(
        functools.partial(_wkv_body, Gb, Gq, Tc, U, state is not None),
        grid=(B // Gb, nq // Gq, T // Tc),
        in_specs=specs,
        out_specs=[seq, st],
        out_shape=[jax.ShapeDtypeStruct((B, T, D), F32), jax.ShapeDtypeStruct((B, H, N, N), F32)],
        scratch_shapes=[pltpu.VMEM((G, N, QL), F32)] + ([rows] * 5 + [pltpu.VMEM((G, U * N, QL), F32)]) * n_buf,
        compiler_params=_params(("arbitrary", "arbitrary", "arbitrary")),
        name="wkv",
    )(*operands)


def _rwkv_layer(x, shift, wkv_state, prm, ln_g, ln_b, *, tm, wkv_cfg):
    mu, w_rkv, w1, w2, a1, a2, g1, g2, vec, r_k, w_o = prm
    B, T, D = x.shape
    w_rkv = jnp.stack([_fold_lanes(w_rkv[0]), _fold_lanes(w_rkv[1]), w_rkv[2]])
    vec_f = jnp.concatenate([_fold_lanes(vec[0:2]), vec[2:]], axis=0)
    r, k, v, d, a, g = _rwkv_proj(x, shift, mu, w_rkv, w1, _fold_lanes(w2), a1, _fold_lanes(a2), g1, g2, vec_f, tm=tm)
    vec8 = jnp.concatenate([_fold_lanes(vec[2:4]), vec[4:6], _fold_lanes(r_k.reshape(1, D)), jnp.zeros((3, D), F32)],
                           axis=0)
    y, s_new = _wkv(r, k, v, d, a, vec8, wkv_state, **wkv_cfg)
    x_new = _proj_norm(x.reshape(B * T, D), y.reshape(B * T, D), g.reshape(B * T, D), w_o,
                       jnp.zeros((D,), F32), ln_g, ln_b, tm=tm)
    return x_new.reshape(B, T, D), s_new, x[:, -1]


CONV_HIST = CONV_WIDTH - 1
CONV_HALO = 32


def _glu_body(x_ref, w_ref, b_ref, u_ref):
    h = _dot(x_ref[...], w_ref[...]) + b_ref[...]
    D = u_ref.shape[-1]
    u_ref[...] = h[:, :D] * _sigmoid(h[:, D:])


def _glu(x2, w, b, *, tm):
    M, D = x2.shape
    tm = min(tm, M)
    assert M % tm == 0
    return pl.pallas_call(
        _glu_body,
        grid=(M // tm,),
        in_specs=[pl.BlockSpec((tm, D), lambda i: (i, 0)),
                  pl.BlockSpec((D, 2 * D), lambda i: (0, 0)),
                  pl.BlockSpec((1, 2 * D), lambda i: (0, 0))],
        out_specs=pl.BlockSpec((tm, D), lambda i: (i, 0)),
        out_shape=jax.ShapeDtypeStruct((M, D), F32),
        compiler_params=_params(("arbitrary",)),
        name="conf_glu",
    )(x2, w, b.reshape(1, 2 * D))


def _conf_tail(conv, x, bdw_ref, cln_ref, w2_ref, b2_ref, g_ref, b_ref):
    cln = cln_ref[...]
    c = _silu(_layer_norm(conv + bdw_ref[...], cln[0:1], cln[1:2]))
    out = _dot(c, w2_ref[...]) + b2_ref[...]
    return _layer_norm(DN_ALPHA * x + out, g_ref[...], b_ref[...])


def _conf_long_body(seq_len, tm, x_ref, u_ref, halo_ref, wdw_ref, bdw_ref, cln_ref, w2_ref, b2_ref, g_ref, b_ref,
                    y_ref, ext_ref):
    i = pl.program_id(0)
    starts_sequence = (i * tm) % seq_len == 0
    ext_ref[0:CONV_HALO, :] = jnp.where(starts_sequence, 0.0, halo_ref[...])
    ext_ref[CONV_HALO:CONV_HALO + tm, :] = u_ref[...]
    wdw = wdw_ref[...]
    first = CONV_HALO - CONV_HIST
    conv = ext_ref[first:first + tm, :] * wdw[0:1]
    for j in range(1, CONV_WIDTH):
        conv = conv + ext_ref[first + j:first + j + tm, :] * wdw[j:j + 1]
    y_ref[...] = _conf_tail(conv, x_ref[...], bdw_ref, cln_ref, w2_ref, b2_ref, g_ref, b_ref)


def _conf_short_body(seq_len, nb, x_ref, u_ref, hist_ref, wdw_ref, bdw_ref, cln_ref, w2_ref, b2_ref, g_ref, b_ref,
                     y_ref, hist_out_ref, ext_ref):
    D = u_ref.shape[-1]
    ext_ref[:, 0:CONV_HIST, :] = hist_ref[...]
    ext_ref[:, CONV_HIST:CONV_HIST + seq_len, :] = u_ref[...].reshape(nb, seq_len, D)
    wdw = wdw_ref[...]
    conv = ext_ref[:, 0:seq_len, :] * wdw[0:1]
    for j in range(1, CONV_WIDTH):
        conv = conv + ext_ref[:, j:j + seq_len, :] * wdw[j:j + 1]
    hist_out_ref[...] = ext_ref[:, seq_len:seq_len + CONV_HIST, :]
    y_ref[...] = _conf_tail(conv.reshape(nb * seq_len, D), x_ref[...], bdw_ref, cln_ref, w2_ref, b2_ref, g_ref, b_ref)


def _conformer_layer(x, hist, prm, ln_g, ln_b, *, tm):
    w_pw1, b_pw1, w_dw, b_dw, cln, w_pw2, b_pw2 = prm
    B, T, D = x.shape
    M = B * T
    x2 = x.reshape(M, D)
    u = _glu(x2, w_pw1, b_pw1, tm=tm)
    const = lambda r, c: pl.BlockSpec((r, c), lambda i: (0, 0))
    weights = [w_dw, b_dw.reshape(1, D), cln, w_pw2, b_pw2.reshape(1, D), ln_g.reshape(1, D), ln_b.reshape(1, D)]
    wspecs = [const(CONV_WIDTH, D), const(1, D), const(2, D), const(D, D), const(1, D), const(1, D), const(1, D)]
    if hist is None:
        tm = min(tm, T)
        assert T % tm == 0 and tm % CONV_HALO == 0
        row = pl.BlockSpec((tm, D), lambda i: (i, 0))
        halo_spec = pl.BlockSpec((CONV_HALO, D), lambda i: (jnp.maximum(i * (tm // CONV_HALO) - 1, 0), 0))
        y = pl.pallas_call(
            functools.partial(_conf_long_body, T, tm),
            grid=(M // tm,),
            in_specs=[row, row, halo_spec] + wspecs,
            out_specs=row,
            out_shape=jax.ShapeDtypeStruct((M, D), F32),
            scratch_shapes=[pltpu.VMEM((CONV_HALO + tm, D), F32)],
            compiler_params=_params(("arbitrary",)),
            name="conf_conv_long",
        )(x2, u, u, *weights)
        new_hist = u.reshape(B, T, D)[:, T - CONV_HIST:, :]
    else:
        assert T == SUBLANES
        nb = min(tm // T, B)
        assert B % nb == 0
        row = pl.BlockSpec((nb * T, D), lambda i: (i, 0))
        hspec = pl.BlockSpec((nb, CONV_HIST, D), lambda i: (i, 0, 0))
        y, new_hist = pl.pallas_call(
            functools.partial(_conf_short_body, T, nb),
            grid=(B // nb,),
            in_specs=[row, row, hspec] + wspecs,
            out_specs=[row, hspec],
            out_shape=[jax.ShapeDtypeStruct((M, D), F32), jax.ShapeDtypeStruct((B, CONV_HIST, D), F32)],
            scratch_shapes=[pltpu.VMEM((nb, CONV_HIST + T + 2, D), F32)],
            compiler_params=_params(("arbitrary",)),
            name="conf_conv_short",
        )(x2, u, hist, *weights)
    return y.reshape(B, T, D), new_hist


def _rms_norm(z, g):
    return z * lax.rsqrt(jnp.mean(z * z, axis=-1, keepdims=True) + RMS_EPS) * g


def _mla_proj_body(keys_bf16, x_ref, cos_ref, sin_ref, wdq_ref, qn_ref, wqn_ref, wqr_ref, wqs_ref, wckv_ref, wkr_ref,
                   wks_ref, kvn_ref, wuk_ref, qlat_ref, qpe_ref, ckv_ref, kpe_ref, *key_copies):
    x = x_ref[...]
    cos, sin = cos_ref[...], sin_ref[...]
    cq = _rms_norm(_dot(x, wdq_ref[...]), qn_ref[...])
    q_nope = _dot(cq, wqn_ref[...])
    q_rope = _dot(cq, wqr_ref[...])
    q_swap = _dot(cq, wqs_ref[...])
    for h in range(MLA_HEADS):
        q_lat = _dot(q_nope[:, h * QK_NOPE:(h + 1) * QK_NOPE], wuk_ref[h])
        qlat_ref[h] = (q_lat * MLA_SCALE).astype(qlat_ref.dtype)
        sl = slice(h * QK_ROPE, (h + 1) * QK_ROPE)
        qpe_ref[h] = ((q_rope[:, sl] * cos + q_swap[:, sl] * sin) * MLA_SCALE).astype(qpe_ref.dtype)
    ckv = _rms_norm(_dot(x, wckv_ref[...]), kvn_ref[...])
    kpe = _dot(x, wkr_ref[...]) * cos + _dot(x, wks_ref[...]) * sin
    ckv_ref[...] = ckv
    kpe_ref[...] = kpe
    if keys_bf16:
        key_copies[0][...] = ckv.astype(BF16)
        key_copies[1][...] = kpe.astype(BF16)


def _swap_halves(w):
    half = w.shape[-1] // 2
    return jnp.concatenate([w[..., half:], w[..., :half]], axis=-1)


def _rope_tables(pos):
    half = QK_ROPE // 2
    inv = ROPE_THETA ** (-jnp.arange(half, dtype=F32) / half)
    ang = pos[:, None] * inv[None, :]
    cos, sin = jnp.cos(ang), jnp.sin(ang)
    return jnp.concatenate([cos, cos], -1), jnp.concatenate([-sin, sin], -1)


def _mla_proj(x2, pos, w_dq, q_norm, w_uq, w_dkv, kv_norm, w_uk, *, tm, keys_bf16):
    M, D = x2.shape
    H = MLA_HEADS
    tm = min(tm, M)
    assert M % tm == 0
    cos, sin = _rope_tables(pos)
    w_qn = w_uq[:, :, :QK_NOPE].reshape(Q_LORA, H * QK_NOPE)
    w_qr = w_uq[:, :, QK_NOPE:].reshape(Q_LORA, H * QK_ROPE)
    w_qs = _swap_halves(w_uq[:, :, QK_NOPE:]).reshape(Q_LORA, H * QK_ROPE)
    w_ckv, w_kr = w_dkv[:, :KV_LORA], w_dkv[:, KV_LORA:]
    w_ks = _swap_halves(w_kr)
    w_ukT = jnp.transpose(w_uk, (1, 2, 0))
    full = lambda arr: pl.BlockSpec(arr.shape, lambda i: (0,) * arr.ndim)
    row = lambda c: pl.BlockSpec((tm, c), lambda i: (i, 0))
    hrow = lambda c: pl.BlockSpec((H, tm, c), lambda i: (0, i, 0))
    weights = [w_dq, q_norm.reshape(1, Q_LORA), w_qn, w_qr, w_qs, w_ckv, w_kr, w_ks, kv_norm.reshape(1, KV_LORA), w_ukT]
    qdt = BF16 if keys_bf16 else F32
    out_specs = [hrow(KV_LORA), hrow(QK_ROPE), row(KV_LORA), row(QK_ROPE)]
    out_shape = [jax.ShapeDtypeStruct((H, M, KV_LORA), qdt), jax.ShapeDtypeStruct((H, M, QK_ROPE), qdt),
                 jax.ShapeDtypeStruct((M, KV_LORA), F32), jax.ShapeDtypeStruct((M, QK_ROPE), F32)]
    if keys_bf16:
        out_specs += [row(KV_LORA), row(QK_ROPE)]
        out_shape += [jax.ShapeDtypeStruct((M, KV_LORA), BF16), jax.ShapeDtypeStruct((M, QK_ROPE), BF16)]
    return pl.pallas_call(
        functools.partial(_mla_proj_body, keys_bf16),
        grid=(M // tm,),
        in_specs=[row(D), row(QK_ROPE), row(QK_ROPE)] + [full(w) for w in weights],
        out_specs=out_specs,
        out_shape=out_shape,
        compiler_params=_params(("arbitrary",)),
        name="mla_proj",
    )(x2, cos, sin, *weights)


def _scores(ql, qp, ckv, kpe):
    return _dot_t(ql, ckv) + _dot_t(qp, kpe)


def _softmax_step(s, values, m_ref, l_ref, acc_ref):
    m_old = m_ref[...]
    m_new = jnp.maximum(m_old, jnp.max(s, axis=-1, keepdims=True))
    p = jnp.exp(s - m_new)
    alpha = jnp.exp(m_old - m_new)
    l_ref[...] = alpha * l_ref[...] + jnp.sum(p, axis=-1, keepdims=True)
    pv = functools.reduce(jnp.add, [_dot(p[:, ks], v) for ks, v in values])
    acc_ref[...] = alpha * acc_ref[...] + pv
    m_ref[...] = m_new


HEAD_GROUP = 1


def _flash_body(tq, tk, qi_ref, kj_ref, qlat_ref, qpe_ref, ckv_ref, kpe_ref, o_ref, m_ref, l_ref, acc_ref):
    n = pl.program_id(1)
    i, j = qi_ref[n], kj_ref[n]
    H = MLA_HEADS
    last_j = ((i + 1) * tq - 1) // tk

    @pl.when(j == 0)
    def _():
        m_ref[...] = jnp.full(m_ref.shape, -jnp.inf, F32)
        l_ref[...] = jnp.zeros(l_ref.shape, F32)
        acc_ref[...] = jnp.zeros(acc_ref.shape, F32)

    def update(masked):
        ckv, kpe = ckv_ref[...], kpe_ref[...]
        for h0 in range(0, H, HEAD_GROUP):
            hs = slice(h0, h0 + HEAD_GROUP)
            rows = slice(h0 * tq, (h0 + HEAD_GROUP) * tq)
            s = _scores(qlat_ref[hs].reshape(HEAD_GROUP * tq, KV_LORA), qpe_ref[hs].reshape(HEAD_GROUP * tq, QK_ROPE),
                        ckv, kpe)
            if masked:
                q_pos = i * tq + lax.broadcasted_iota(jnp.int32, (HEAD_GROUP * tq, 1), 0) % tq
                k_pos = j * tk + lax.broadcasted_iota(jnp.int32, (1, tk), 1)
                s = jnp.where(k_pos <= q_pos, s, -jnp.inf)
            _softmax_step(s, [(slice(None), ckv)], m_ref.at[rows], l_ref.at[rows], acc_ref.at[rows])

    @pl.when(j < last_j)
    def _():
        update(False)

    @pl.when(j == last_j)
    def _():
        update(True)
        o_ref[...] = (acc_ref[...] / l_ref[...]).reshape(H, tq, KV_LORA)


def _flash_attention(q_lat, q_pe, ckv, kpe, B, T, *, tq, tk):
    H = MLA_HEADS
    tq, tk = min(tq, T), min(tk, T)
    assert T % tq == 0 and T % tk == 0 and tk % tq == 0
    nq, nk = T // tq, T // tk
    pairs = [(i, j) for i in range(nq) for j in range(((i + 1) * tq - 1) // tk + 1)]
    qi = jnp.asarray([p[0] for p in pairs], jnp.int32)
    kj = jnp.asarray([p[1] for p in pairs], jnp.int32)
    qmap = lambda b, n, qi, kj: (0, b * nq + qi[n], 0)
    kmap = lambda b, n, qi, kj: (b * nk + kj[n], 0)
    grid_spec = pltpu.PrefetchScalarGridSpec(
        num_scalar_prefetch=2,
        grid=(B, len(pairs)),
        in_specs=[pl.BlockSpec((H, tq, KV_LORA), qmap), pl.BlockSpec((H, tq, QK_ROPE), qmap),
                  pl.BlockSpec((tk, KV_LORA), kmap), pl.BlockSpec((tk, QK_ROPE), kmap)],
        out_specs=pl.BlockSpec((H, tq, KV_LORA), qmap),
        scratch_shapes=[pltpu.VMEM((H * tq, 1), F32), pltpu.VMEM((H * tq, 1), F32), pltpu.VMEM((H * tq, KV_LORA), F32)],
    )
    return pl.pallas_call(
        functools.partial(_flash_body, tq, tk),
        grid_spec=grid_spec,
        out_shape=jax.ShapeDtypeStruct((H, B * T, KV_LORA), F32),
        compiler_params=_params(("arbitrary", "arbitrary")),
        name="mla_flash",
    )(qi, kj, q_lat, q_pe, ckv, kpe)


def _paged_body(PG, T, page, pt_ref, qlat_ref, qpe_ref, ckvn_ref, kpen_ref, *rest):
    pages_ckv, pages_kpe = rest[:PG], rest[PG:2 * PG]
    o_ref, m_ref, l_ref, acc_ref, selfk_ref, selfp_ref = rest[2 * PG:]
    g = pl.program_id(1)
    H = MLA_HEADS
    ql = qlat_ref[...].reshape(H * T, KV_LORA).astype(BF16)
    qp = qpe_ref[...].reshape(H * T, QK_ROPE).astype(BF16)

    @pl.when(g == 0)
    def _():
        m_ref[...] = jnp.full(m_ref.shape, -jnp.inf, F32)
        l_ref[...] = jnp.zeros(l_ref.shape, F32)
        acc_ref[...] = jnp.zeros(acc_ref.shape, F32)

    keys = jnp.concatenate([pages_ckv[p][0, 0].astype(BF16) for p in range(PG)], axis=0)
    rope_keys_t = jnp.concatenate([pages_kpe[p][0, 0].astype(BF16) for p in range(PG)], axis=1)
    _softmax_step(_dot_t(ql, keys) + _dot(qp, rope_keys_t), [(slice(None), keys)], m_ref, l_ref, acc_ref)

    @pl.when(g == pl.num_programs(1) - 1)
    def _():
        selfk_ref[...] = jnp.zeros(selfk_ref.shape, F32)
        selfp_ref[...] = jnp.zeros(selfp_ref.shape, F32)
        selfk_ref[0:T, :] = ckvn_ref[...]
        selfp_ref[0:T, :] = kpen_ref[...]
        ckv = selfk_ref[...]
        s = _scores(ql, qp, ckv, selfp_ref[...])
        q_pos = lax.broadcasted_iota(jnp.int32, (H * T, 1), 0) % T
        k_pos = lax.broadcasted_iota(jnp.int32, (1, page), 1)
        s = jnp.where(k_pos <= q_pos, s, -jnp.inf)
        _softmax_step(s, [(slice(None), ckv)], m_ref, l_ref, acc_ref)
        o_ref[...] = (acc_ref[...] / l_ref[...]).reshape(H, T, KV_LORA)


def _paged_attention(q_lat, q_pe, ckv, kpe, cache_ckv, cache_kpe_t, page_table, layer, T, *, PG):
    H = MLA_HEADS
    DB, n_pages = page_table.shape
    page = cache_ckv.shape[2]
    assert n_pages % PG == 0 and T <= page
    qmap = lambda b, g, pt: (0, b, 0)
    nmap = lambda b, g, pt: (b, 0)
    pmap = lambda p: (lambda b, g, pt: (layer, pt[b * n_pages + g * PG + p], 0, 0))
    grid_spec = pltpu.PrefetchScalarGridSpec(
        num_scalar_prefetch=1,
        grid=(DB, n_pages // PG),
        in_specs=[pl.BlockSpec((H, T, KV_LORA), qmap), pl.BlockSpec((H, T, QK_ROPE), qmap),
                  pl.BlockSpec((T, KV_LORA), nmap), pl.BlockSpec((T, QK_ROPE), nmap)]
                 + [pl.BlockSpec((1, 1, page, KV_LORA), pmap(p)) for p in range(PG)]
                 + [pl.BlockSpec((1, 1, QK_ROPE, page), pmap(p)) for p in range(PG)],
        out_specs=pl.BlockSpec((H, T, KV_LORA), qmap),
        scratch_shapes=[pltpu.VMEM((H * T, 1), F32), pltpu.VMEM((H * T, 1), F32), pltpu.VMEM((H * T, KV_LORA), F32),
                        pltpu.VMEM((page, KV_LORA), F32), pltpu.VMEM((page, QK_ROPE), F32)],
    )
    return pl.pallas_call(
        functools.partial(_paged_body, PG, T, page),
        grid_spec=grid_spec,
        out_shape=jax.ShapeDtypeStruct((H, DB * T, KV_LORA), F32),
        compiler_params=_params(("arbitrary", "arbitrary")),
        name="mla_paged",
    )(page_table.reshape(-1), q_lat, q_pe, ckv, kpe, *([cache_ckv] * PG), *([cache_kpe_t] * PG))


def _mla_out_body(x_ref, o_ref, wuv_ref, wo_ref, g_ref, b_ref, y_ref):
    heads = [_dot(o_ref[h], wuv_ref[h]) for h in range(MLA_HEADS)]
    out = _dot(jnp.concatenate(heads, axis=-1), wo_ref[...])
    y_ref[...] = _layer_norm(DN_ALPHA * x_ref[...] + out, g_ref[...], b_ref[...])


def _mla_out(x2, o_lat, w_uv, w_o, ln_g, ln_b, *, tm):
    M, D = x2.shape
    H = MLA_HEADS
    tm = min(tm, M)
    assert M % tm == 0
    w_uvh = jnp.transpose(w_uv, (1, 0, 2))
    const = lambda shape: pl.BlockSpec(shape, lambda i: (0,) * len(shape))
    return pl.pallas_call(
        _mla_out_body,
        grid=(M // tm,),
        in_specs=[pl.BlockSpec((tm, D), lambda i: (i, 0)), pl.BlockSpec((H, tm, KV_LORA), lambda i: (0, i, 0)),
                  const(w_uvh.shape), const(w_o.shape), const((1, D)), const((1, D))],
        out_specs=pl.BlockSpec((tm, D), lambda i: (i, 0)),
        out_shape=jax.ShapeDtypeStruct((M, D), F32),
        compiler_params=_params(("arbitrary",)),
        name="mla_out",
    )(x2, o_lat, w_uvh, w_o, ln_g.reshape(1, D), ln_b.reshape(1, D))


def _mla_layer(x, pos0, cache, prm, ln_g, ln_b, *, tm, attn_cfg):
    w_dq, q_norm, w_uq, w_dkv, kv_norm, w_uk, w_uv, w_o = prm
    B, T, D = x.shape
    x2 = x.reshape(B * T, D)
    pos = jnp.tile(pos0 + jnp.arange(T, dtype=F32), B)
    proj = _mla_proj(x2, pos, w_dq, q_norm, w_uq, w_dkv, kv_norm, w_uk, tm=tm, keys_bf16=cache is None)
    q_lat, q_pe, ckv, kpe = proj[:4]
    if cache is None:
        o_lat = _flash_attention(q_lat, q_pe, proj[4], proj[5], B, T, **attn_cfg)
    else:
        o_lat = _paged_attention(q_lat, q_pe, ckv, kpe, *cache, T, **attn_cfg)
    y = _mla_out(x2, o_lat, w_uv, w_o, ln_g, ln_b, tm=tm)
    return y.reshape(B, T, D), ckv.reshape(B, T, KV_LORA), kpe.reshape(B, T, QK_ROPE)


def kernel(x_prompt, x_sample, state_wkv, state_shift, state_conv, cache_mla_ckv, cache_mla_kpe, state_ffn_conv, page_table, ln_g, ln_b, rwkv_mu, rwkv_w_rkv, rwkv_w1, rwkv_w2, rwkv_a1, rwkv_a2, rwkv_g1, rwkv_g2, rwkv_vec, rwkv_r_k, rwkv_w_o, conf_w_pw1, conf_b_pw1, conf_w_dw, conf_b_dw, conf_ln, conf_w_pw2, conf_b_pw2, mla_w_dq, mla_q_norm, mla_w_uq, mla_w_dkv, mla_kv_norm, mla_w_uk, mla_w_uv, mla_w_o, ffn_w_in, ffn_w_dw, ffn_b_dw, ffn_w_out):
    bf = lambda w: w.astype(BF16)
    past_len = page_table.shape[1] * cache_mla_ckv.shape[2]
    xp, xs = x_prompt, x_sample
    out_p = dict(wkv=[], shift=[], conv=[], ckv=[], kpe=[], ffn=[])
    out_s = dict(wkv=[], shift=[], conv=[], ckv=[], kpe=[], ffn=[])
    for i in range(DEPTH):
        j, kind = i // N_MIXERS, i % N_MIXERS
        g0, b0, g1, b1 = ln_g[i, 0], ln_b[i, 0], ln_g[i, 1], ln_b[i, 1]
        if kind == 0:
            prm = (rwkv_mu[j], bf(rwkv_w_rkv[j]), bf(rwkv_w1[j]), bf(rwkv_w2[j]), bf(rwkv_a1[j]), bf(rwkv_a2[j]),
                   bf(rwkv_g1[j]), bf(rwkv_g2[j]), rwkv_vec[j], rwkv_r_k[j], bf(rwkv_w_o[j]))
            xp, st, sh = _rwkv_layer(xp, None, None, prm, g0, b0, tm=512, wkv_cfg=dict(Gb=2, Gq=4, Tc=128, U=16))
            out_p["wkv"].append(st)
            out_p["shift"].append(sh)
            xs, st, sh = _rwkv_layer(xs, state_shift[j], state_wkv[j], prm, g0, b0, tm=512,
                                     wkv_cfg=dict(Gb=2, Gq=4, Tc=8, U=8))
            out_s["wkv"].append(st)
            out_s["shift"].append(sh)
        elif kind == 1:
            prm = (bf(conf_w_pw1[j]), conf_b_pw1[j], conf_w_dw[j], conf_b_dw[j], conf_ln[j], bf(conf_w_pw2[j]),
                   conf_b_pw2[j])
            xp, cb = _conformer_layer(xp, None, prm, g0, b0, tm=256)
            out_p["conv"].append(cb)
            xs, cb = _conformer_layer(xs, state_conv[j], prm, g0, b0, tm=256)
            out_s["conv"].append(cb)
        else:
            prm = (bf(mla_w_dq[j]), mla_q_norm[j], bf(mla_w_uq[j]), bf(mla_w_dkv[j]), mla_kv_norm[j],
                   bf(mla_w_uk[j]), bf(mla_w_uv[j]), bf(mla_w_o[j]))
            xp, ckv, kpe = _mla_layer(xp, 0.0, None, prm, g0, b0, tm=512, attn_cfg=dict(tq=256, tk=1024))
            out_p["ckv"].append(ckv)
            out_p["kpe"].append(kpe)
            xs, ckv, kpe = _mla_layer(xs, float(past_len), (cache_mla_ckv, jnp.swapaxes(cache_mla_kpe, 2, 3), page_table, j), prm,
                                      g0, b0, tm=512, attn_cfg=dict(PG=32))
            out_s["ckv"].append(ckv)
            out_s["kpe"].append(kpe)
        w_in, w_out = bf(ffn_w_in[i]), bf(ffn_w_out[i])
        xp, fb = _conv_ffn(xp, None, w_in, ffn_w_dw[i], ffn_b_dw[i], w_out, g1, b1, tm=512, fc=1408)
        out_p["ffn"].append(fb)
        xs, fb = _conv_ffn(xs, state_ffn_conv[i], w_in, ffn_w_dw[i], ffn_b_dw[i], w_out, g1, b1, tm=512, fc=1408)
        out_s["ffn"].append(fb)
    names = ("wkv", "shift", "conv", "ckv", "kpe", "ffn")
    return (xp, xs) + tuple(jnp.stack(out_p[n]) for n in names) + tuple(jnp.stack(out_s[n]) for n in names)
```

```python
import functools

import jax
import jax.numpy as jnp
from jax import lax
from jax.experimental import pallas as pl
from jax.experimental.pallas import tpu as pltpu

D_MODEL = 1024
DEPTH = 4
N_MIXERS = 3
DN_ALPHA = (2 * DEPTH) ** 0.25
LN_EPS = 1e-5
RWKV_HEAD = 64
RWKV_HEADS = D_MODEL // RWKV_HEAD
GN_EPS = 64e-5
CONV_WIDTH = 31
MLA_HEADS = 8
QK_NOPE = 128
QK_ROPE = 64
V_HEAD = 128
KV_LORA = 256
Q_LORA = 384
ROPE_THETA = 10000.0
MLA_SCALE = (QK_NOPE + QK_ROPE) ** -0.5
RMS_EPS = 1e-6
D_FF = 2816
FFN_CONV_WIDTH = 3

SUBLANES = 8
LANES = 128
VMEM_LIMIT = 56 * 1024 * 1024

BF16 = jnp.bfloat16
F32 = jnp.float32


def _params(sem):
    return pltpu.CompilerParams(dimension_semantics=sem, vmem_limit_bytes=VMEM_LIMIT)


def _dot(a, b):
    return jnp.dot(a.astype(BF16), b.astype(BF16), preferred_element_type=F32)


def _dot_t(a, b):
    return lax.dot_general(a.astype(BF16), b.astype(BF16), (((1,), (1,)), ((), ())),
                           preferred_element_type=F32)


def _layer_norm(z, g, b, eps=LN_EPS):
    mu = jnp.mean(z, axis=-1, keepdims=True)
    zc = z - mu
    var = jnp.mean(zc * zc, axis=-1, keepdims=True)
    return zc * lax.rsqrt(var + eps) * g + b


def _sigmoid(z):
    return 1.0 / (1.0 + jnp.exp(-z))


def _silu(z):
    return z * _sigmoid(z)


def _prev_rows(cur, shift, pos, fill):
    out = pltpu.roll(cur, shift, 0)
    for p in range(shift):
        out = jnp.where(pos == p, fill[shift - 1 - p], out)
    return out


def _ffn_body(seq_len, tm, x_ref, halo_ref, wa_ref, wb_ref, wdw_ref, bdw_ref, wo_ref, g_ref, b_ref,
              y_ref, tail_ref, acc_ref, xb_ref):
    i, f = pl.program_id(0), pl.program_id(1)

    @pl.when(f == 0)
    def _():
        xb_ref[...] = x_ref[...].astype(BF16)

    xb = xb_ref[...]
    a = _dot(xb, wa_ref[...])
    b = _dot(xb, wb_ref[...])
    fc = a.shape[-1]
    rows = lax.broadcasted_iota(jnp.int32, (tm, 1), 0)
    if seq_len >= tm:
        starts_sequence = (i * tm) % seq_len == 0
        ah = _dot(halo_ref[...], wa_ref[...])
        ah = jnp.where(starts_sequence, 0.0, ah)
        pos = rows
        before1, before2 = ah[SUBLANES - 1:SUBLANES], ah[SUBLANES - 2:SUBLANES - 1]
    else:
        hist = halo_ref[...]
        nseq = tm // seq_len
        expand = lambda r: jnp.broadcast_to(r, (nseq, seq_len, fc)).reshape(tm, fc)
        pos = rows % seq_len
        before1, before2 = expand(hist[:, 1:2, :]), expand(hist[:, 0:1, :])
    a1 = _prev_rows(a, 1, pos, [before1])
    a2 = _prev_rows(a, 2, pos, [before1, before2])
    wdw = wdw_ref[...]
    c = a2 * wdw[0:1] + a1 * wdw[1:2] + a * wdw[2:3] + bdw_ref[...]
    part = _dot(_silu(c) * b, wo_ref[...])

    @pl.when(f == 0)
    def _():
        acc_ref[...] = part

    @pl.when(f > 0)
    def _():
        acc_ref[...] += part

    if seq_len >= tm:
        tail_ref[0] = a[tm - SUBLANES:tm]
    else:
        tail_ref[...] = a.reshape(tm // seq_len, seq_len, fc)

    @pl.when(f == pl.num_programs(1) - 1)
    def _():
        y_ref[...] = _layer_norm(DN_ALPHA * x_ref[...] + acc_ref[...], g_ref[...], b_ref[...])


def _conv_ffn(x, hist, w_in, w_dw, b_dw, w_out, ln_g, ln_b, *, tm, fc):
    B, T, D = x.shape
    F = w_out.shape[0]
    M = B * T
    x2 = x.reshape(M, D)
    nF = F // fc
    long_seq = hist is None
    if long_seq:
        tm = min(tm, T)
        assert T % tm == 0 and tm % SUBLANES == 0
        halo = x2
        halo_spec = pl.BlockSpec((SUBLANES, D), lambda i, f: (jnp.maximum(i * (tm // SUBLANES) - 1, 0), 0))
        tail_shape = jax.ShapeDtypeStruct((M // tm, SUBLANES, F), F32)
        tail_spec = pl.BlockSpec((1, SUBLANES, fc), lambda i, f: (i, 0, f))
    else:
        assert T == SUBLANES
        tm = min(tm, M)
        assert M % tm == 0 and tm % T == 0
        halo = hist
        halo_spec = pl.BlockSpec((tm // T, FFN_CONV_WIDTH - 1, fc), lambda i, f: (i, 0, f))
        tail_shape = jax.ShapeDtypeStruct((B, T, F), F32)
        tail_spec = pl.BlockSpec((tm // T, T, fc), lambda i, f: (i, 0, f))
    y, tail = pl.pallas_call(
        functools.partial(_ffn_body, T, tm),
        grid=(M // tm, nF),
        in_specs=[
            pl.BlockSpec((tm, D), lambda i, f: (i, 0)),
            halo_spec,
            pl.BlockSpec((D, fc), lambda i, f: (0, f)),
            pl.BlockSpec((D, fc), lambda i, f: (0, nF + f)),
            pl.BlockSpec((FFN_CONV_WIDTH, fc), lambda i, f: (0, f)),
            pl.BlockSpec((1, fc), lambda i, f: (0, f)),
            pl.BlockSpec((fc, D), lambda i, f: (f, 0)),
            pl.BlockSpec((1, D), lambda i, f: (0, 0)),
            pl.BlockSpec((1, D), lambda i, f: (0, 0)),
        ],
        out_specs=[pl.BlockSpec((tm, D), lambda i, f: (i, 0)), tail_spec],
        out_shape=[jax.ShapeDtypeStruct((M, D), F32), tail_shape],
        scratch_shapes=[pltpu.VMEM((tm, D), F32), pltpu.VMEM((tm, D), BF16)],
        compiler_params=_params(("arbitrary", "arbitrary")),
        name="conv_ffn",
    )(x2, halo, w_in, w_in, w_dw, b_dw.reshape(1, F), w_out, ln_g.reshape(1, D), ln_b.reshape(1, D))
    if long_seq:
        tail = tail[T // tm - 1::T // tm]
    return y.reshape(B, T, D), tail[:, -(FFN_CONV_WIDTH - 1):, :]


def _proj_norm_body(gated, *refs):
    if gated:
        x_ref, h_ref, gate_ref, w_ref, bias_ref, g_ref, b_ref, y_ref = refs
        h = h_ref[...] * gate_ref[...]
    else:
        x_ref, h_ref, w_ref, bias_ref, g_ref, b_ref, y_ref = refs
        h = h_ref[...]
    out = _dot(h, w_ref[...]) + bias_ref[...]
    y_ref[...] = _layer_norm(DN_ALPHA * x_ref[...] + out, g_ref[...], b_ref[...])


def _proj_norm(x, h, gate, w, bias, ln_g, ln_b, *, tm):
    M, D = x.shape
    K = h.shape[1]
    tm = min(tm, M)
    assert M % tm == 0
    row = lambda c: pl.BlockSpec((tm, c), lambda i: (i, 0))
    const = lambda r, c: pl.BlockSpec((r, c), lambda i: (0, 0))
    operands = [x, h] + ([gate] if gate is not None else []) + [w, bias.reshape(1, D), ln_g.reshape(1, D), ln_b.reshape(1, D)]
    specs = [row(D), row(K)] + ([row(K)] if gate is not None else []) + [const(K, D), const(1, D), const(1, D), const(1, D)]
    return pl.pallas_call(
        functools.partial(_proj_norm_body, gate is not None),
        grid=(M // tm,),
        in_specs=specs,
        out_specs=row(D),
        out_shape=jax.ShapeDtypeStruct((M, D), F32),
        compiler_params=_params(("arbitrary",)),
        name="proj_norm",
    )(*operands)


def _softplus(z):
    return jnp.maximum(z, 0.0) + jnp.log(1.0 + jnp.exp(-jnp.abs(z)))


def _rwkv_proj_body(seq_len, tm, x_ref, halo_ref, mu_ref, wrkv_ref, w1_ref, w2_ref, a1_ref, a2_ref,
                    g1_ref, g2_ref, vec_ref, r_ref, k_ref, v_ref, d_ref, a_ref, g_ref):
    i = pl.program_id(0)
    x = x_ref[...]
    D = x.shape[-1]
    rows = lax.broadcasted_iota(jnp.int32, (tm, 1), 0)
    if seq_len >= tm:
        starts_sequence = (i * tm) % seq_len == 0
        before = jnp.where(starts_sequence, 0.0, halo_ref[SUBLANES - 1:SUBLANES, :])
        pos = rows
    else:
        nseq = tm // seq_len
        before = jnp.broadcast_to(halo_ref[...], (nseq, seq_len, D)).reshape(tm, D)
        pos = rows % seq_len
    dx = _prev_rows(x, 1, pos, [before]) - x
    mu = mu_ref[...]
    mix = lambda j: x + dx * mu[j:j + 1]
    vec = vec_ref[...]
    r_ref[...] = _dot(mix(0), wrkv_ref[0])
    k_ref[...] = _dot(mix(1), wrkv_ref[1])
    v_ref[...] = _dot(mix(2), wrkv_ref[2])
    w_pre = vec[0:1] + _dot(jnp.tanh(_dot(mix(3), w1_ref[...])), w2_ref[...])
    d_ref[...] = jnp.exp(-jnp.exp(-_softplus(-w_pre) - 0.5))
    a_ref[...] = _sigmoid(vec[1:2] + _dot(_dot(mix(4), a1_ref[...]), a2_ref[...]))
    g_ref[...] = _dot(_sigmoid(_dot(mix(5), g1_ref[...])), g2_ref[...])


def _rwkv_proj(x, shift, mu, w_rkv, w1, w2, a1, a2, g1, g2, vec, *, tm):
    B, T, D = x.shape
    M = B * T
    x2 = x.reshape(M, D)
    if shift is None:
        tm = min(tm, T)
        assert T % tm == 0
        halo = x2
        halo_spec = pl.BlockSpec((SUBLANES, D), lambda i: (jnp.maximum(i * (tm // SUBLANES) - 1, 0), 0))
    else:
        assert T == SUBLANES
        tm = min(tm, M)
        assert M % tm == 0
        halo = shift.reshape(B, 1, D)
        halo_spec = pl.BlockSpec((tm // T, 1, D), lambda i: (i, 0, 0))
    full = lambda arr: pl.BlockSpec(arr.shape, lambda i: (0,) * arr.ndim)
    row = pl.BlockSpec((tm, D), lambda i: (i, 0))
    outs = pl.pallas_call(
        functools.partial(_rwkv_proj_body, T, tm),
        grid=(M // tm,),
        in_specs=[row, halo_spec] + [full(w) for w in (mu, w_rkv, w1, w2, a1, a2, g1, g2, vec)],
        out_specs=[row] * 6,
        out_shape=[jax.ShapeDtypeStruct((M, D), F32)] * 6,
        compiler_params=_params(("arbitrary",)),
        name="rwkv_proj",
    )(x2, halo, mu, w_rkv, w1, w2, a1, a2, g1, g2, vec)
    return [o.reshape(B, T, D) for o in outs]


QUAD = 4
QUAD_LANES = QUAD * RWKV_HEAD
HALF_HEAD = RWKV_HEAD // 2
MXU_DEPTH = 256


def _fold_lanes(w):
    lead = w.shape[:-1]
    w = w.reshape(lead + (RWKV_HEADS // QUAD, QUAD, 2, HALF_HEAD))
    return jnp.swapaxes(w, -3, -2).reshape(lead + (RWKV_HEADS * RWKV_HEAD,))


def _split3(z):
    z1 = z.astype(BF16)
    r1 = z - z1.astype(F32)
    z2 = r1.astype(BF16)
    z3 = (r1 - z2.astype(F32)).astype(BF16)
    return z1, z2, z3


def _seg_sum_mxu(z, ones):
    n = z.shape[0]
    s = jnp.dot(jnp.concatenate(_split3(z), axis=0), ones, preferred_element_type=F32)
    return s[0:n] + s[n:2 * n] + s[2 * n:3 * n]


def _wkv_body(Gb, Gq, Tc, U, has_state, r_ref, k_ref, v_ref, d_ref, a_ref, vec_ref, *rest):
    if has_state:
        s0_ref, y_ref, sT_ref, S_scr, *bufs = rest
    else:
        y_ref, sT_ref, S_scr, *bufs = rest
    N, QL, HL = RWKV_HEAD, QUAD_LANES, LANES
    av_scr, bv_scr, dv_scr, rv_scr, bonus_scr, ycol_scr, vk_scr = [bufs[i::7] for i in range(7)]
    ic = pl.program_id(2)
    tiles = [(gb, gq) for gb in range(Gb) for gq in range(Gq)]
    G = len(tiles)
    n_parts = 3 * QUAD * U
    KP = LANES if n_parts <= LANES else MXU_DEPTH
    iota = lambda shape, axis: lax.broadcasted_iota(jnp.int32, shape, axis)
    head_folded = lambda lane: (lane % HL) // HALF_HEAD
    head_plain = lambda lane: lane // N
    lane_slot = iota((N, HL), 1) % HALF_HEAD
    lane_step = iota((N, KP), 1) % U
    ri, ci = iota((QL, QL), 0), iota((QL, QL), 1)
    block_ones = lambda same: jnp.where(same, 1.0, 0.0).astype(BF16)
    ones_ff = block_ones(head_folded(ri) == head_folded(ci))
    ones_fp = block_ones(head_folded(ri) == head_plain(ci))
    ones_pp = block_ones(head_plain(ri) == head_plain(ci))
    ones_step = block_ones((ri // HL == ci // HL) & (head_folded(ri) == head_folded(ci)))
    pad_rows = jnp.zeros((KP - n_parts, QL), F32)
    no_rows = jnp.zeros((N, HL), BF16)

    @pl.when(ic == 0)
    def _():
        for c, (gb, gq) in enumerate(tiles):
            if has_state:
                S_scr[c] = jnp.concatenate([s0_ref[gb, QUAD * gq + h][:, HALF_HEAD * half:HALF_HEAD * (half + 1)]
                                            for half in range(2) for h in range(QUAD)], axis=-1)
            else:
                S_scr[c] = jnp.zeros((N, QL), F32)

    def head_rows(z, head_of_lane):
        heads = head_of_lane(iota((U, QL), 1))
        return [jnp.where(heads == h, z, 0.0) for h in range(QUAD)]

    def half_sum(z):
        return (z[:, 0:HL] + z[:, HL:QL]).astype(BF16)

    def at(c, off):
        gb, gq = tiles[c]
        return (gb, pl.ds(off, U), slice(QL * gq, QL * (gq + 1)))

    def vec_of(c):
        return vec_ref[:, QL * tiles[c][1]:QL * (tiles[c][1] + 1)]

    def k_hat_of(c, off):
        return k_ref[at(c, off)] * (1.0 + (a_ref[at(c, off)] - 1.0) * vec_of(c)[1:2])

    def prepare_rows(off, slot):
        k_scaled, bonus_in = [], []
        for c in range(G):
            vec = vec_of(c)
            k_scaled.append(k_ref[at(c, off)] * vec[0:1])
            bonus_in.append(r_ref[at(c, off)] * k_hat_of(c, off) * vec[4:5])
        norms = _seg_sum_mxu(jnp.concatenate([z * z for z in k_scaled], axis=0), ones_ff)
        bonus = _seg_sum_mxu(jnp.concatenate(bonus_in, axis=0), ones_fp)
        for c in range(G):
            kk = k_scaled[c] / jnp.maximum(jnp.sqrt(norms[c * U:(c + 1) * U]), 1e-12)
            bonus_scr[slot][c] = bonus[c * U:(c + 1) * U] * v_ref[at(c, off)]
            av_scr[slot][c] = -kk
            bv_scr[slot][c] = kk * a_ref[at(c, off)]
            dv_scr[slot][c] = d_ref[at(c, off)]
            rv_scr[slot][c] = r_ref[at(c, off)]

    def prepare_outer(off, slot, c):
        v = v_ref[at(c, off)]
        k_hat = k_hat_of(c, off)
        v_hi = v.astype(BF16).astype(F32)
        v_lo = v - v_hi
        k_hi = k_hat.astype(BF16).astype(F32)
        k_lo = k_hat - k_hi
        v_stack = jnp.concatenate(head_rows(v_hi, head_plain) + head_rows(v_hi, head_plain)
                                  + head_rows(v_lo, head_plain) + [pad_rows], axis=0)
        v_t = v_stack.T
        v_cols = functools.reduce(jnp.add, [v_t[N * h:N * (h + 1)] for h in range(QUAD)]).astype(BF16)
        k_rows = jnp.concatenate(head_rows(k_hi, head_folded) + head_rows(k_lo, head_folded)
                                 + head_rows(k_hi, head_folded) + [pad_rows], axis=0).astype(BF16)
        lhs = jnp.concatenate([jnp.where(lane_step == t, v_cols, jnp.zeros_like(v_cols)) for t in range(U)], axis=0)
        vk_scr[slot][c] = jnp.dot(lhs, k_rows, preferred_element_type=F32)

    def preparation(off, slot):
        return [functools.partial(prepare_rows, off, slot)] + [functools.partial(prepare_outer, off, slot, c)
                                                               for c in range(G)]

    def recur(off, slot, side_work=()):
        side_at = {}
        for n, piece in enumerate(side_work):
            side_at.setdefault((n * U) // len(side_work), []).append(piece)
        S = [S_scr[c] for c in range(G)]
        y_cols = [jnp.zeros((N, HL), F32) for _ in tiles]
        for t in range(U + 1):
            blocks = []
            for c in range(G):
                s_bf = S[c].astype(BF16)
                sa_in = half_sum(s_bf * av_scr[slot][c, t:t + 1, :].astype(BF16)) if t < U else no_rows
                y_in = half_sum(s_bf * rv_scr[slot][c, t - 1:t, :].astype(BF16)) if t > 0 else no_rows
                blocks.append(jnp.concatenate([sa_in, y_in], axis=1))
            out = jnp.dot(jnp.concatenate(blocks, axis=0), ones_step, preferred_element_type=F32)
            for piece in side_at.get(t, ()):
                piece()
            for c in range(G):
                o = out[N * c:N * (c + 1)]
                if t > 0:
                    y_cols[c] = jnp.where(lane_slot == t - 1, o[:, HL:QL], y_cols[c])
                if t < U:
                    sa = jnp.concatenate([o[:, 0:HL], o[:, 0:HL]], axis=1)
                    S[c] = (S[c] * dv_scr[slot][c, t:t + 1, :] + sa * bv_scr[slot][c, t:t + 1, :]
                            + vk_scr[slot][c, N * t:N * (t + 1), :])
        for c in range(G):
            S_scr[c] = S[c]
            ycol_scr[slot][c] = y_cols[c]

    def emit(off, slot):
        ys = []
        for c in range(G):
            y_t = ycol_scr[slot][c].T
            ys.append(jnp.concatenate([y_t[HALF_HEAD * h:HALF_HEAD * h + U, :] for h in range(QUAD)], axis=-1))
        y_all = jnp.concatenate(ys, axis=0)
        yc_all = y_all - _seg_sum_mxu(y_all, ones_pp) * (1.0 / N)
        var_all = _seg_sum_mxu(yc_all * yc_all, ones_pp) * (1.0 / N)
        yn_all = yc_all * lax.rsqrt(var_all + GN_EPS)
        for c in range(G):
            vec = vec_of(c)
            y_ref[at(c, off)] = yn_all[c * U:(c + 1) * U] * vec[2:3] + vec[3:4] + bonus_scr[slot][c]

    n_sub = Tc // U
    for piece in preparation(0, 0):
        piece()
    if n_sub == 1:
        recur(0, 0)
        emit(0, 0)
    else:
        for c in range(G):
            ycol_scr[1][c] = jnp.zeros((N, HL), F32)
            bonus_scr[1][c] = jnp.zeros((U, QL), F32)

        def pair(jj, carry):
            off0 = pl.multiple_of(jj * (2 * U), 2 * U)
            off1 = pl.multiple_of(off0 + U, U)
            off2 = pl.multiple_of(jnp.minimum(off0 + 2 * U, Tc - U), U)
            before = pl.multiple_of(jnp.maximum(off0 - U, 0), U)
            recur(off0, 0, [functools.partial(emit, before, 1)] + preparation(off1, 1))
            recur(off1, 1, [functools.partial(emit, off0, 0)] + preparation(off2, 0))
            return carry

        lax.fori_loop(0, n_sub // 2, pair, 0)
        emit(Tc - U, 1)

    @pl.when(ic == pl.num_programs(2) - 1)
    def _():
        for c, (gb, gq) in enumerate(tiles):
            Sc = S_scr[c]
            for h in range(QUAD):
                sT_ref[gb, QUAD * gq + h] = jnp.concatenate(
                    [Sc[:, HALF_HEAD * h:HALF_HEAD * (h + 1)], Sc[:, HL + HALF_HEAD * h:HL + HALF_HEAD * (h + 1)]], axis=1)


def _wkv(r, k, v, d, a, vec8, state, *, Gb, Gq, Tc, U):
    B, T, D = r.shape
    H, N, QL = RWKV_HEADS, RWKV_HEAD, QUAD_LANES
    Tc = min(Tc, T)
    U = min(U, Tc)
    nq = H // QUAD
    assert 3 * QUAD * U <= MXU_DEPTH and U % SUBLANES == 0 and U <= HALF_HEAD
    assert T % Tc == 0 and Tc % U == 0 and (Tc == U or (Tc // U) % 2 == 0) and B % Gb == 0 and nq % Gq == 0
    G = Gb * Gq
    seq = pl.BlockSpec((Gb, Tc, QL * Gq), lambda b, p, c: (b, c, p))
    st = pl.BlockSpec((Gb, QUAD * Gq, N, N), lambda b, p, c: (b, p, 0, 0))
    operands = [r, k, v, d, a, vec8] + ([state] if state is not None else [])
    specs = [seq] * 5 + [pl.BlockSpec((8, QL * Gq), lambda b, p, c: (0, p))] + ([st] if state is not None else [])
    n_buf = 1 if Tc == U else 2
    rows = pltpu.VMEM((G, U, QL), F32)
    return pl.pallas_call(
        functools.partial(_wkv_body, Gb, Gq, Tc, U, state is not None),
        grid=(B // Gb, nq // Gq, T // Tc),
        in_specs=specs,
        out_specs=[seq, st],
        out_shape=[jax.ShapeDtypeStruct((B, T, D), F32), jax.ShapeDtypeStruct((B, H, N, N), F32)],
        scratch_shapes=[pltpu.VMEM((G, N, QL), F32)] + ([rows] * 5 + [pltpu.VMEM((G, N, LANES), F32),
                                                                      pltpu.VMEM((G, U * N, QL), F32)]) * n_buf,
        compiler_params=_params(("arbitrary", "arbitrary", "arbitrary")),
        name="wkv",
    )(*operands)


def _rwkv_layer(x, shift, wkv_state, prm, ln_g, ln_b, *, tm, wkv_cfg):
    mu, w_rkv, w1, w2, a1, a2, g1, g2, vec, r_k, w_o = prm
    B, T, D = x.shape
    w_rkv = jnp.stack([_fold_lanes(w_rkv[0]), _fold_lanes(w_rkv[1]), w_rkv[2]])
    vec_f = jnp.concatenate([_fold_lanes(vec[0:2]), vec[2:]], axis=0)
    r, k, v, d, a, g = _rwkv_proj(x, shift, mu, w_rkv, w1, _fold_lanes(w2), a1, _fold_lanes(a2), g1, g2, vec_f, tm=tm)
    vec8 = jnp.concatenate([_fold_lanes(vec[2:4]), vec[4:6], _fold_lanes(r_k.reshape(1, D)), jnp.zeros((3, D), F32)],
                           axis=0)
    y, s_new = _wkv(r, k, v, d, a, vec8, wkv_state, **wkv_cfg)
    x_new = _proj_norm(x.reshape(B * T, D), y.reshape(B * T, D), g.reshape(B * T, D), w_o,
                       jnp.zeros((D,), F32), ln_g, ln_b, tm=tm)
    return x_new.reshape(B, T, D), s_new, x[:, -1]


CONV_HIST = CONV_WIDTH - 1
CONV_HALO = 32


def _glu_body(x_ref, w_ref, b_ref, u_ref):
    h = _dot(x_ref[...], w_ref[...]) + b_ref[...]
    D = u_ref.shape[-1]
    u_ref[...] = h[:, :D] * _sigmoid(h[:, D:])


def _glu(x2, w, b, *, tm):
    M, D = x2.shape
    tm = min(tm, M)
    assert M % tm == 0
    return pl.pallas_call(
        _glu_body,
        grid=(M // tm,),
        in_specs=[pl.BlockSpec((tm, D), lambda i: (i, 0)),
                  pl.BlockSpec((D, 2 * D), lambda i: (0, 0)),
                  pl.BlockSpec((1, 2 * D), lambda i: (0, 0))],
        out_specs=pl.BlockSpec((tm, D), lambda i: (i, 0)),
        out_shape=jax.ShapeDtypeStruct((M, D), F32),
        compiler_params=_params(("arbitrary",)),
        name="conf_glu",
    )(x2, w, b.reshape(1, 2 * D))


def _conf_tail(conv, x, bdw_ref, cln_ref, w2_ref, b2_ref, g_ref, b_ref):
    cln = cln_ref[...]
    c = _silu(_layer_norm(conv + bdw_ref[...], cln[0:1], cln[1:2]))
    out = _dot(c, w2_ref[...]) + b2_ref[...]
    return _layer_norm(DN_ALPHA * x + out, g_ref[...], b_ref[...])


def _conf_long_body(seq_len, tm, x_ref, u_ref, halo_ref, wdw_ref, bdw_ref, cln_ref, w2_ref, b2_ref, g_ref, b_ref,
                    y_ref, ext_ref):
    i = pl.program_id(0)
    starts_sequence = (i * tm) % seq_len == 0
    ext_ref[0:CONV_HALO, :] = jnp.where(starts_sequence, 0.0, halo_ref[...])
    ext_ref[CONV_HALO:CONV_HALO + tm, :] = u_ref[...]
    wdw = wdw_ref[...]
    first = CONV_HALO - CONV_HIST
    conv = ext_ref[first:first + tm, :] * wdw[0:1]
    for j in range(1, CONV_WIDTH):
        conv = conv + ext_ref[first + j:first + j + tm, :] * wdw[j:j + 1]
    y_ref[...] = _conf_tail(conv, x_ref[...], bdw_ref, cln_ref, w2_ref, b2_ref, g_ref, b_ref)


def _conf_short_body(seq_len, nb, x_ref, u_ref, hist_ref, wdw_ref, bdw_ref, cln_ref, w2_ref, b2_ref, g_ref, b_ref,
                     y_ref, hist_out_ref, ext_ref):
    D = u_ref.shape[-1]
    ext_ref[:, 0:CONV_HIST, :] = hist_ref[...]
    ext_ref[:, CONV_HIST:CONV_HIST + seq_len, :] = u_ref[...].reshape(nb, seq_len, D)
    wdw = wdw_ref[...]
    conv = ext_ref[:, 0:seq_len, :] * wdw[0:1]
    for j in range(1, CONV_WIDTH):
        conv = conv + ext_ref[:, j:j + seq_len, :] * wdw[j:j + 1]
    hist_out_ref[...] = ext_ref[:, seq_len:seq_len + CONV_HIST, :]
    y_ref[...] = _conf_tail(conv.reshape(nb * seq_len, D), x_ref[...], bdw_ref, cln_ref, w2_ref, b2_ref, g_ref, b_ref)


def _conformer_layer(x, hist, prm, ln_g, ln_b, *, tm):
    w_pw1, b_pw1, w_dw, b_dw, cln, w_pw2, b_pw2 = prm
    B, T, D = x.shape
    M = B * T
    x2 = x.reshape(M, D)
    u = _glu(x2, w_pw1, b_pw1, tm=tm)
    const = lambda r, c: pl.BlockSpec((r, c), lambda i: (0, 0))
    weights = [w_dw, b_dw.reshape(1, D), cln, w_pw2, b_pw2.reshape(1, D), ln_g.reshape(1, D), ln_b.reshape(1, D)]
    wspecs = [const(CONV_WIDTH, D), const(1, D), const(2, D), const(D, D), const(1, D), const(1, D), const(1, D)]
    if hist is None:
        tm = min(tm, T)
        assert T % tm == 0 and tm % CONV_HALO == 0
        row = pl.BlockSpec((tm, D), lambda i: (i, 0))
        halo_spec = pl.BlockSpec((CONV_HALO, D), lambda i: (jnp.maximum(i * (tm // CONV_HALO) - 1, 0), 0))
        y = pl.pallas_call(
            functools.partial(_conf_long_body, T, tm),
            grid=(M // tm,),
            in_specs=[row, row, halo_spec] + wspecs,
            out_specs=row,
            out_shape=jax.ShapeDtypeStruct((M, D), F32),
            scratch_shapes=[pltpu.VMEM((CONV_HALO + tm, D), F32)],
            compiler_params=_params(("arbitrary",)),
            name="conf_conv_long",
        )(x2, u, u, *weights)
        new_hist = u.reshape(B, T, D)[:, T - CONV_HIST:, :]
    else:
        assert T == SUBLANES
        nb = min(tm // T, B)
        assert B % nb == 0
        row = pl.BlockSpec((nb * T, D), lambda i: (i, 0))
        hspec = pl.BlockSpec((nb, CONV_HIST, D), lambda i: (i, 0, 0))
        y, new_hist = pl.pallas_call(
            functools.partial(_conf_short_body, T, nb),
            grid=(B // nb,),
            in_specs=[row, row, hspec] + wspecs,
            out_specs=[row, hspec],
            out_shape=[jax.ShapeDtypeStruct((M, D), F32), jax.ShapeDtypeStruct((B, CONV_HIST, D), F32)],
            scratch_shapes=[pltpu.VMEM((nb, CONV_HIST + T + 2, D), F32)],
            compiler_params=_params(("arbitrary",)),
            name="conf_conv_short",
        )(x2, u, hist, *weights)
    return y.reshape(B, T, D), new_hist


def _rms_norm(z, g):
    return z * lax.rsqrt(jnp.mean(z * z, axis=-1, keepdims=True) + RMS_EPS) * g


def _mla_proj_body(keys_bf16, x_ref, cos_ref, sin_ref, wdq_ref, qn_ref, wqn_ref, wqr_ref, wqs_ref, wckv_ref, wkr_ref,
                   wks_ref, kvn_ref, wuk_ref, qlat_ref, qpe_ref, ckv_ref, kpe_ref, *key_copies):
    x = x_ref[...]
    cos, sin = cos_ref[...], sin_ref[...]
    cq = _rms_norm(_dot(x, wdq_ref[...]), qn_ref[...])
    q_nope = _dot(cq, wqn_ref[...])
    q_rope = _dot(cq, wqr_ref[...])
    q_swap = _dot(cq, wqs_ref[...])
    for h in range(MLA_HEADS):
        q_lat = _dot(q_nope[:, h * QK_NOPE:(h + 1) * QK_NOPE], wuk_ref[h])
        qlat_ref[h] = (q_lat * MLA_SCALE).astype(qlat_ref.dtype)
        sl = slice(h * QK_ROPE, (h + 1) * QK_ROPE)
        qpe_ref[h] = ((q_rope[:, sl] * cos + q_swap[:, sl] * sin) * MLA_SCALE).astype(qpe_ref.dtype)
    ckv = _rms_norm(_dot(x, wckv_ref[...]), kvn_ref[...])
    kpe = _dot(x, wkr_ref[...]) * cos + _dot(x, wks_ref[...]) * sin
    ckv_ref[...] = ckv
    kpe_ref[...] = kpe
    if keys_bf16:
        key_copies[0][...] = ckv.astype(BF16)
        key_copies[1][...] = kpe.astype(BF16)


def _swap_halves(w):
    half = w.shape[-1] // 2
    return jnp.concatenate([w[..., half:], w[..., :half]], axis=-1)


def _rope_tables(pos):
    half = QK_ROPE // 2
    inv = ROPE_THETA ** (-jnp.arange(half, dtype=F32) / half)
    ang = pos[:, None] * inv[None, :]
    cos, sin = jnp.cos(ang), jnp.sin(ang)
    return jnp.concatenate([cos, cos], -1), jnp.concatenate([-sin, sin], -1)


def _mla_proj(x2, pos, w_dq, q_norm, w_uq, w_dkv, kv_norm, w_uk, *, tm, keys_bf16):
    M, D = x2.shape
    H = MLA_HEADS
    tm = min(tm, M)
    assert M % tm == 0
    cos, sin = _rope_tables(pos)
    w_qn = w_uq[:, :, :QK_NOPE].reshape(Q_LORA, H * QK_NOPE)
    w_qr = w_uq[:, :, QK_NOPE:].reshape(Q_LORA, H * QK_ROPE)
    w_qs = _swap_halves(w_uq[:, :, QK_NOPE:]).reshape(Q_LORA, H * QK_ROPE)
    w_ckv, w_kr = w_dkv[:, :KV_LORA], w_dkv[:, KV_LORA:]
    w_ks = _swap_halves(w_kr)
    w_ukT = jnp.transpose(w_uk, (1, 2, 0))
    full = lambda arr: pl.BlockSpec(arr.shape, lambda i: (0,) * arr.ndim)
    row = lambda c: pl.BlockSpec((tm, c), lambda i: (i, 0))
    hrow = lambda c: pl.BlockSpec((H, tm, c), lambda i: (0, i, 0))
    weights = [w_dq, q_norm.reshape(1, Q_LORA), w_qn, w_qr, w_qs, w_ckv, w_kr, w_ks, kv_norm.reshape(1, KV_LORA), w_ukT]
    qdt = BF16 if keys_bf16 else F32
    out_specs = [hrow(KV_LORA), hrow(QK_ROPE), row(KV_LORA), row(QK_ROPE)]
    out_shape = [jax.ShapeDtypeStruct((H, M, KV_LORA), qdt), jax.ShapeDtypeStruct((H, M, QK_ROPE), qdt),
                 jax.ShapeDtypeStruct((M, KV_LORA), F32), jax.ShapeDtypeStruct((M, QK_ROPE), F32)]
    if keys_bf16:
        out_specs += [row(KV_LORA), row(QK_ROPE)]
        out_shape += [jax.ShapeDtypeStruct((M, KV_LORA), BF16), jax.ShapeDtypeStruct((M, QK_ROPE), BF16)]
    return pl.pallas_call(
        functools.partial(_mla_proj_body, keys_bf16),
        grid=(M // tm,),
        in_specs=[row(D), row(QK_ROPE), row(QK_ROPE)] + [full(w) for w in weights],
        out_specs=out_specs,
        out_shape=out_shape,
        compiler_params=_params(("arbitrary",)),
        name="mla_proj",
    )(x2, cos, sin, *weights)


def _scores(ql, qp, ckv, kpe):
    return _dot_t(ql, ckv) + _dot_t(qp, kpe)


def _softmax_step(s, values, m_ref, l_ref, acc_ref):
    m_old = m_ref[...]
    m_new = jnp.maximum(m_old, jnp.max(s, axis=-1, keepdims=True))
    p = jnp.exp(s - m_new)
    alpha = jnp.exp(m_old - m_new)
    l_ref[...] = alpha * l_ref[...] + jnp.sum(p, axis=-1, keepdims=True)
    pv = functools.reduce(jnp.add, [_dot(p[:, ks], v) for ks, v in values])
    acc_ref[...] = alpha * acc_ref[...] + pv
    m_ref[...] = m_new


HEAD_GROUP = 1


def _flash_body(tq, tk, qi_ref, kj_ref, qlat_ref, qpe_ref, ckv_ref, kpe_ref, o_ref, m_ref, l_ref, acc_ref):
    n = pl.program_id(1)
    i, j = qi_ref[n], kj_ref[n]
    H = MLA_HEADS
    last_j = ((i + 1) * tq - 1) // tk

    @pl.when(j == 0)
    def _():
        m_ref[...] = jnp.full(m_ref.shape, -jnp.inf, F32)
        l_ref[...] = jnp.zeros(l_ref.shape, F32)
        acc_ref[...] = jnp.zeros(acc_ref.shape, F32)

    def update(masked):
        ckv, kpe = ckv_ref[...], kpe_ref[...]
        for h0 in range(0, H, HEAD_GROUP):
            hs = slice(h0, h0 + HEAD_GROUP)
            rows = slice(h0 * tq, (h0 + HEAD_GROUP) * tq)
            s = _scores(qlat_ref[hs].reshape(HEAD_GROUP * tq, KV_LORA), qpe_ref[hs].reshape(HEAD_GROUP * tq, QK_ROPE),
                        ckv, kpe)
            if masked:
                q_pos = i * tq + lax.broadcasted_iota(jnp.int32, (HEAD_GROUP * tq, 1), 0) % tq
                k_pos = j * tk + lax.broadcasted_iota(jnp.int32, (1, tk), 1)
                s = jnp.where(k_pos <= q_pos, s, -jnp.inf)
            _softmax_step(s, [(slice(None), ckv)], m_ref.at[rows], l_ref.at[rows], acc_ref.at[rows])

    @pl.when(j < last_j)
    def _():
        update(False)

    @pl.when(j == last_j)
    def _():
        update(True)
        o_ref[...] = (acc_ref[...] / l_ref[...]).reshape(H, tq, KV_LORA)


def _flash_attention(q_lat, q_pe, ckv, kpe, B, T, *, tq, tk):
    H = MLA_HEADS
    tq, tk = min(tq, T), min(tk, T)
    assert T % tq == 0 and T % tk == 0 and tk % tq == 0
    nq, nk = T // tq, T // tk
    pairs = [(i, j) for i in range(nq) for j in range(((i + 1) * tq - 1) // tk + 1)]
    qi = jnp.asarray([p[0] for p in pairs], jnp.int32)
    kj = jnp.asarray([p[1] for p in pairs], jnp.int32)
    qmap = lambda b, n, qi, kj: (0, b * nq + qi[n], 0)
    kmap = lambda b, n, qi, kj: (b * nk + kj[n], 0)
    grid_spec = pltpu.PrefetchScalarGridSpec(
        num_scalar_prefetch=2,
        grid=(B, len(pairs)),
        in_specs=[pl.BlockSpec((H, tq, KV_LORA), qmap), pl.BlockSpec((H, tq, QK_ROPE), qmap),
                  pl.BlockSpec((tk, KV_LORA), kmap), pl.BlockSpec((tk, QK_ROPE), kmap)],
        out_specs=pl.BlockSpec((H, tq, KV_LORA), qmap),
        scratch_shapes=[pltpu.VMEM((H * tq, 1), F32), pltpu.VMEM((H * tq, 1), F32), pltpu.VMEM((H * tq, KV_LORA), F32)],
    )
    return pl.pallas_call(
        functools.partial(_flash_body, tq, tk),
        grid_spec=grid_spec,
        out_shape=jax.ShapeDtypeStruct((H, B * T, KV_LORA), F32),
        compiler_params=_params(("arbitrary", "arbitrary")),
        name="mla_flash",
    )(qi, kj, q_lat, q_pe, ckv, kpe)


def _paged_body(PG, T, page, pt_ref, qlat_ref, qpe_ref, ckvn_ref, kpen_ref, *rest):
    pages_ckv, pages_kpe = rest[:PG], rest[PG:2 * PG]
    o_ref, m_ref, l_ref, acc_ref, selfk_ref, selfp_ref = rest[2 * PG:]
    g = pl.program_id(1)
    H = MLA_HEADS
    ql = qlat_ref[...].reshape(H * T, KV_LORA).astype(BF16)
    qp = qpe_ref[...].reshape(H * T, QK_ROPE).astype(BF16)

    @pl.when(g == 0)
    def _():
        m_ref[...] = jnp.full(m_ref.shape, -jnp.inf, F32)
        l_ref[...] = jnp.zeros(l_ref.shape, F32)
        acc_ref[...] = jnp.zeros(acc_ref.shape, F32)

    keys = jnp.concatenate([pages_ckv[p][0, 0].astype(BF16) for p in range(PG)], axis=0)
    rope_keys_t = jnp.concatenate([pages_kpe[p][0, 0].astype(BF16) for p in range(PG)], axis=1)
    _softmax_step(_dot_t(ql, keys) + _dot(qp, rope_keys_t), [(slice(None), keys)], m_ref, l_ref, acc_ref)

    @pl.when(g == pl.num_programs(1) - 1)
    def _():
        selfk_ref[...] = jnp.zeros(selfk_ref.shape, F32)
        selfp_ref[...] = jnp.zeros(selfp_ref.shape, F32)
        selfk_ref[0:T, :] = ckvn_ref[...]
        selfp_ref[0:T, :] = kpen_ref[...]
        ckv = selfk_ref[...]
        s = _scores(ql, qp, ckv, selfp_ref[...])
        q_pos = lax.broadcasted_iota(jnp.int32, (H * T, 1), 0) % T
        k_pos = lax.broadcasted_iota(jnp.int32, (1, page), 1)
        s = jnp.where(k_pos <= q_pos, s, -jnp.inf)
        _softmax_step(s, [(slice(None), ckv)], m_ref, l_ref, acc_ref)
        o_ref[...] = (acc_ref[...] / l_ref[...]).reshape(H, T, KV_LORA)


def _paged_attention(q_lat, q_pe, ckv, kpe, cache_ckv, cache_kpe_t, page_table, layer, T, *, PG):
    H = MLA_HEADS
    DB, n_pages = page_table.shape
    page = cache_ckv.shape[2]
    assert n_pages % PG == 0 and T <= page
    qmap = lambda b, g, pt: (0, b, 0)
    nmap = lambda b, g, pt: (b, 0)
    pmap = lambda p: (lambda b, g, pt: (layer, pt[b * n_pages + g * PG + p], 0, 0))
    grid_spec = pltpu.PrefetchScalarGridSpec(
        num_scalar_prefetch=1,
        grid=(DB, n_pages // PG),
        in_specs=[pl.BlockSpec((H, T, KV_LORA), qmap), pl.BlockSpec((H, T, QK_ROPE), qmap),
                  pl.BlockSpec((T, KV_LORA), nmap), pl.BlockSpec((T, QK_ROPE), nmap)]
                 + [pl.BlockSpec((1, 1, page, KV_LORA), pmap(p)) for p in range(PG)]
                 + [pl.BlockSpec((1, 1, QK_ROPE, page), pmap(p)) for p in range(PG)],
        out_specs=pl.BlockSpec((H, T, KV_LORA), qmap),
        scratch_shapes=[pltpu.VMEM((H * T, 1), F32), pltpu.VMEM((H * T, 1), F32), pltpu.VMEM((H * T, KV_LORA), F32),
                        pltpu.VMEM((page, KV_LORA), F32), pltpu.VMEM((page, QK_ROPE), F32)],
    )
    return pl.pallas_call(
        functools.partial(_paged_body, PG, T, page),
        grid_spec=grid_spec,
        out_shape=jax.ShapeDtypeStruct((H, DB * T, KV_LORA), F32),
        compiler_params=_params(("arbitrary", "arbitrary")),
        name="mla_paged",
    )(page_table.reshape(-1), q_lat, q_pe, ckv, kpe, *([cache_ckv] * PG), *([cache_kpe_t] * PG))


def _mla_out_body(x_ref, o_ref, wuv_ref, wo_ref, g_ref, b_ref, y_ref):
    heads = [_dot(o_ref[h], wuv_ref[h]) for h in range(MLA_HEADS)]
    out = _dot(jnp.concatenate(heads, axis=-1), wo_ref[...])
    y_ref[...] = _layer_norm(DN_ALPHA * x_ref[...] + out, g_ref[...], b_ref[...])


def _mla_out(x2, o_lat, w_uv, w_o, ln_g, ln_b, *, tm):
    M, D = x2.shape
    H = MLA_HEADS
    tm = min(tm, M)
    assert M % tm == 0
    w_uvh = jnp.transpose(w_uv, (1, 0, 2))
    const = lambda shape: pl.BlockSpec(shape, lambda i: (0,) * len(shape))
    return pl.pallas_call(
        _mla_out_body,
        grid=(M // tm,),
        in_specs=[pl.BlockSpec((tm, D), lambda i: (i, 0)), pl.BlockSpec((H, tm, KV_LORA), lambda i: (0, i, 0)),
                  const(w_uvh.shape), const(w_o.shape), const((1, D)), const((1, D))],
        out_specs=pl.BlockSpec((tm, D), lambda i: (i, 0)),
        out_shape=jax.ShapeDtypeStruct((M, D), F32),
        compiler_params=_params(("arbitrary",)),
        name="mla_out",
    )(x2, o_lat, w_uvh, w_o, ln_g.reshape(1, D), ln_b.reshape(1, D))


def _mla_layer(x, pos0, cache, prm, ln_g, ln_b, *, tm, attn_cfg):
    w_dq, q_norm, w_uq, w_dkv, kv_norm, w_uk, w_uv, w_o = prm
    B, T, D = x.shape
    x2 = x.reshape(B * T, D)
    pos = jnp.tile(pos0 + jnp.arange(T, dtype=F32), B)
    proj = _mla_proj(x2, pos, w_dq, q_norm, w_uq, w_dkv, kv_norm, w_uk, tm=tm, keys_bf16=cache is None)
    q_lat, q_pe, ckv, kpe = proj[:4]
    if cache is None:
        o_lat = _flash_attention(q_lat, q_pe, proj[4], proj[5], B, T, **attn_cfg)
    else:
        o_lat = _paged_attention(q_lat, q_pe, ckv, kpe, *cache, T, **attn_cfg)
    y = _mla_out(x2, o_lat, w_uv, w_o, ln_g, ln_b, tm=tm)
    return y.reshape(B, T, D), ckv.reshape(B, T, KV_LORA), kpe.reshape(B, T, QK_ROPE)


def kernel(x_prompt, x_sample, state_wkv, state_shift, state_conv, cache_mla_ckv, cache_mla_kpe, state_ffn_conv, page_table, ln_g, ln_b, rwkv_mu, rwkv_w_rkv, rwkv_w1, rwkv_w2, rwkv_a1, rwkv_a2, rwkv_g1, rwkv_g2, rwkv_vec, rwkv_r_k, rwkv_w_o, conf_w_pw1, conf_b_pw1, conf_w_dw, conf_b_dw, conf_ln, conf_w_pw2, conf_b_pw2, mla_w_dq, mla_q_norm, mla_w_uq, mla_w_dkv, mla_kv_norm, mla_w_uk, mla_w_uv, mla_w_o, ffn_w_in, ffn_w_dw, ffn_b_dw, ffn_w_out):
    bf = lambda w: w.astype(BF16)
    past_len = page_table.shape[1] * cache_mla_ckv.shape[2]
    xp, xs = x_prompt, x_sample
    out_p = dict(wkv=[], shift=[], conv=[], ckv=[], kpe=[], ffn=[])
    out_s = dict(wkv=[], shift=[], conv=[], ckv=[], kpe=[], ffn=[])
    for i in range(DEPTH):
        j, kind = i // N_MIXERS, i % N_MIXERS
        g0, b0, g1, b1 = ln_g[i, 0], ln_b[i, 0], ln_g[i, 1], ln_b[i, 1]
        if kind == 0:
            prm = (rwkv_mu[j], bf(rwkv_w_rkv[j]), bf(rwkv_w1[j]), bf(rwkv_w2[j]), bf(rwkv_a1[j]), bf(rwkv_a2[j]),
                   bf(rwkv_g1[j]), bf(rwkv_g2[j]), rwkv_vec[j], rwkv_r_k[j], bf(rwkv_w_o[j]))
            xp, st, sh = _rwkv_layer(xp, None, None, prm, g0, b0, tm=512, wkv_cfg=dict(Gb=2, Gq=4, Tc=256, U=16))
            out_p["wkv"].append(st)
            out_p["shift"].append(sh)
            xs, st, sh = _rwkv_layer(xs, state_shift[j], state_wkv[j], prm, g0, b0, tm=512,
                                     wkv_cfg=dict(Gb=2, Gq=4, Tc=8, U=8))
            out_s["wkv"].append(st)
            out_s["shift"].append(sh)
        elif kind == 1:
            prm = (bf(conf_w_pw1[j]), conf_b_pw1[j], conf_w_dw[j], conf_b_dw[j], conf_ln[j], bf(conf_w_pw2[j]),
                   conf_b_pw2[j])
            xp, cb = _conformer_layer(xp, None, prm, g0, b0, tm=256)
            out_p["conv"].append(cb)
            xs, cb = _conformer_layer(xs, state_conv[j], prm, g0, b0, tm=256)
            out_s["conv"].append(cb)
        else:
            prm = (bf(mla_w_dq[j]), mla_q_norm[j], bf(mla_w_uq[j]), bf(mla_w_dkv[j]), mla_kv_norm[j],
                   bf(mla_w_uk[j]), bf(mla_w_uv[j]), bf(mla_w_o[j]))
            xp, ckv, kpe = _mla_layer(xp, 0.0, None, prm, g0, b0, tm=512, attn_cfg=dict(tq=256, tk=1024))
            out_p["ckv"].append(ckv)
            out_p["kpe"].append(kpe)
            xs, ckv, kpe = _mla_layer(xs, float(past_len), (cache_mla_ckv, jnp.swapaxes(cache_mla_kpe, 2, 3), page_table, j), prm,
                                      g0, b0, tm=512, attn_cfg=dict(PG=32))
            out_s["ckv"].append(ckv)
            out_s["kpe"].append(kpe)
        w_in, w_out = bf(ffn_w_in[i]), bf(ffn_w_out[i])
        xp, fb = _conv_ffn(xp, None, w_in, ffn_w_dw[i], ffn_b_dw[i], w_out, g1, b1, tm=512, fc=1408)
        out_p["ffn"].append(fb)
        xs, fb = _conv_ffn(xs, state_ffn_conv[i], w_in, ffn_w_dw[i], ffn_b_dw[i], w_out, g1, b1, tm=512, fc=1408)
        out_s["ffn"].append(fb)
    names = ("wkv", "shift", "conv", "ckv", "kpe", "ffn")
    return (xp, xs) + tuple(jnp.stack(out_p[n]) for n in names) + tuple(jnp.stack(out_s[n]) for n in names)
```

```python
import functools

import jax
import jax.numpy as jnp
from jax import lax
from jax.experimental import pallas as pl
from jax.experimental.pallas import tpu as pltpu

D_MODEL = 1024
DEPTH = 4
N_MIXERS = 3
DN_ALPHA = (2 * DEPTH) ** 0.25
LN_EPS = 1e-5
RWKV_HEAD = 64
RWKV_HEADS = D_MODEL // RWKV_HEAD
GN_EPS = 64e-5
CONV_WIDTH = 31
MLA_HEADS = 8
QK_NOPE = 128
QK_ROPE = 64
V_HEAD = 128
KV_LORA = 256
Q_LORA = 384
ROPE_THETA = 10000.0
MLA_SCALE = (QK_NOPE + QK_ROPE) ** -0.5
RMS_EPS = 1e-6
D_FF = 2816
FFN_CONV_WIDTH = 3

SUBLANES = 8
LANES = 128
VMEM_LIMIT = 56 * 1024 * 1024

BF16 = jnp.bfloat16
F32 = jnp.float32


def _params(sem):
    return pltpu.CompilerParams(dimension_semantics=sem, vmem_limit_bytes=VMEM_LIMIT)


def _dot(a, b):
    return jnp.dot(a.astype(BF16), b.astype(BF16), preferred_element_type=F32)


def _dot_t(a, b):
    return lax.dot_general(a.astype(BF16), b.astype(BF16), (((1,), (1,)), ((), ())),
                           preferred_element_type=F32)


def _layer_norm(z, g, b, eps=LN_EPS):
    mu = jnp.mean(z, axis=-1, keepdims=True)
    zc = z - mu
    var = jnp.mean(zc * zc, axis=-1, keepdims=True)
    return zc * lax.rsqrt(var + eps) * g + b


def _sigmoid(z):
    return 1.0 / (1.0 + jnp.exp(-z))


def _silu(z):
    return z * _sigmoid(z)


def _prev_rows(cur, shift, pos, fill):
    out = pltpu.roll(cur, shift, 0)
    for p in range(shift):
        out = jnp.where(pos == p, fill[shift - 1 - p], out)
    return out


def _ffn_body(seq_len, tm, x_ref, halo_ref, wa_ref, wb_ref, wdw_ref, bdw_ref, wo_ref, g_ref, b_ref,
              y_ref, tail_ref, acc_ref, xb_ref):
    i, f = pl.program_id(0), pl.program_id(1)

    @pl.when(f == 0)
    def _():
        xb_ref[...] = x_ref[...].astype(BF16)

    xb = xb_ref[...]
    a = _dot(xb, wa_ref[...])
    b = _dot(xb, wb_ref[...])
    fc = a.shape[-1]
    rows = lax.broadcasted_iota(jnp.int32, (tm, 1), 0)
    if seq_len >= tm:
        starts_sequence = (i * tm) % seq_len == 0
        ah = _dot(halo_ref[...], wa_ref[...])
        ah = jnp.where(starts_sequence, 0.0, ah)
        pos = rows
        before1, before2 = ah[SUBLANES - 1:SUBLANES], ah[SUBLANES - 2:SUBLANES - 1]
    else:
        hist = halo_ref[...]
        nseq = tm // seq_len
        expand = lambda r: jnp.broadcast_to(r, (nseq, seq_len, fc)).reshape(tm, fc)
        pos = rows % seq_len
        before1, before2 = expand(hist[:, 1:2, :]), expand(hist[:, 0:1, :])
    a1 = _prev_rows(a, 1, pos, [before1])
    a2 = _prev_rows(a, 2, pos, [before1, before2])
    wdw = wdw_ref[...]
    c = a2 * wdw[0:1] + a1 * wdw[1:2] + a * wdw[2:3] + bdw_ref[...]
    part = _dot(_silu(c) * b, wo_ref[...])

    @pl.when(f == 0)
    def _():
        acc_ref[...] = part

    @pl.when(f > 0)
    def _():
        acc_ref[...] += part

    if seq_len >= tm:
        tail_ref[0] = a[tm - SUBLANES:tm]
    else:
        tail_ref[...] = a.reshape(tm // seq_len, seq_len, fc)

    @pl.when(f == pl.num_programs(1) - 1)
    def _():
        y_ref[...] = _layer_norm(DN_ALPHA * x_ref[...] + acc_ref[...], g_ref[...], b_ref[...])


def _conv_ffn(x, hist, w_in, w_dw, b_dw, w_out, ln_g, ln_b, *, tm, fc):
    B, T, D = x.shape
    F = w_out.shape[0]
    M = B * T
    x2 = x.reshape(M, D)
    nF = F // fc
    long_seq = hist is None
    if long_seq:
        tm = min(tm, T)
        assert T % tm == 0 and tm % SUBLANES == 0
        halo = x2
        halo_spec = pl.BlockSpec((SUBLANES, D), lambda i, f: (jnp.maximum(i * (tm // SUBLANES) - 1, 0), 0))
        tail_shape = jax.ShapeDtypeStruct((M // tm, SUBLANES, F), F32)
        tail_spec = pl.BlockSpec((1, SUBLANES, fc), lambda i, f: (i, 0, f))
    else:
        assert T == SUBLANES
        tm = min(tm, M)
        assert M % tm == 0 and tm % T == 0
        halo = hist
        halo_spec = pl.BlockSpec((tm // T, FFN_CONV_WIDTH - 1, fc), lambda i, f: (i, 0, f))
        tail_shape = jax.ShapeDtypeStruct((B, T, F), F32)
        tail_spec = pl.BlockSpec((tm // T, T, fc), lambda i, f: (i, 0, f))
    y, tail = pl.pallas_call(
        functools.partial(_ffn_body, T, tm),
        grid=(M // tm, nF),
        in_specs=[
            pl.BlockSpec((tm, D), lambda i, f: (i, 0)),
            halo_spec,
            pl.BlockSpec((D, fc), lambda i, f: (0, f)),
            pl.BlockSpec((D, fc), lambda i, f: (0, nF + f)),
            pl.BlockSpec((FFN_CONV_WIDTH, fc), lambda i, f: (0, f)),
            pl.BlockSpec((1, fc), lambda i, f: (0, f)),
            pl.BlockSpec((fc, D), lambda i, f: (f, 0)),
            pl.BlockSpec((1, D), lambda i, f: (0, 0)),
            pl.BlockSpec((1, D), lambda i, f: (0, 0)),
        ],
        out_specs=[pl.BlockSpec((tm, D), lambda i, f: (i, 0)), tail_spec],
        out_shape=[jax.ShapeDtypeStruct((M, D), F32), tail_shape],
        scratch_shapes=[pltpu.VMEM((tm, D), F32), pltpu.VMEM((tm, D), BF16)],
        compiler_params=_params(("arbitrary", "arbitrary")),
        name="conv_ffn",
    )(x2, halo, w_in, w_in, w_dw, b_dw.reshape(1, F), w_out, ln_g.reshape(1, D), ln_b.reshape(1, D))
    if long_seq:
        tail = tail[T // tm - 1::T // tm]
    return y.reshape(B, T, D), tail[:, -(FFN_CONV_WIDTH - 1):, :]


def _proj_norm_body(gated, *refs):
    if gated:
        x_ref, h_ref, gate_ref, w_ref, bias_ref, g_ref, b_ref, y_ref = refs
        h = h_ref[...] * gate_ref[...]
    else:
        x_ref, h_ref, w_ref, bias_ref, g_ref, b_ref, y_ref = refs
        h = h_ref[...]
    out = _dot(h, w_ref[...]) + bias_ref[...]
    y_ref[...] = _layer_norm(DN_ALPHA * x_ref[...] + out, g_ref[...], b_ref[...])


def _proj_norm(x, h, gate, w, bias, ln_g, ln_b, *, tm):
    M, D = x.shape
    K = h.shape[1]
    tm = min(tm, M)
    assert M % tm == 0
    row = lambda c: pl.BlockSpec((tm, c), lambda i: (i, 0))
    const = lambda r, c: pl.BlockSpec((r, c), lambda i: (0, 0))
    operands = [x, h] + ([gate] if gate is not None else []) + [w, bias.reshape(1, D), ln_g.reshape(1, D), ln_b.reshape(1, D)]
    specs = [row(D), row(K)] + ([row(K)] if gate is not None else []) + [const(K, D), const(1, D), const(1, D), const(1, D)]
    return pl.pallas_call(
        functools.partial(_proj_norm_body, gate is not None),
        grid=(M // tm,),
        in_specs=specs,
        out_specs=row(D),
        out_shape=jax.ShapeDtypeStruct((M, D), F32),
        compiler_params=_params(("arbitrary",)),
        name="proj_norm",
    )(*operands)


def _softplus(z):
    return jnp.maximum(z, 0.0) + jnp.log(1.0 + jnp.exp(-jnp.abs(z)))


def _rwkv_proj_body(seq_len, tm, x_ref, halo_ref, mu_ref, wrkv_ref, w1_ref, w2_ref, a1_ref, a2_ref,
                    g1_ref, g2_ref, vec_ref, r_ref, k_ref, v_ref, d_ref, a_ref, g_ref):
    i = pl.program_id(0)
    x = x_ref[...]
    D = x.shape[-1]
    rows = lax.broadcasted_iota(jnp.int32, (tm, 1), 0)
    if seq_len >= tm:
        starts_sequence = (i * tm) % seq_len == 0
        before = jnp.where(starts_sequence, 0.0, halo_ref[SUBLANES - 1:SUBLANES, :])
        pos = rows
    else:
        nseq = tm // seq_len
        before = jnp.broadcast_to(halo_ref[...], (nseq, seq_len, D)).reshape(tm, D)
        pos = rows % seq_len
    dx = _prev_rows(x, 1, pos, [before]) - x
    mu = mu_ref[...]
    mix = lambda j: x + dx * mu[j:j + 1]
    vec = vec_ref[...]
    r_ref[...] = _dot(mix(0), wrkv_ref[0])
    k_ref[...] = _dot(mix(1), wrkv_ref[1])
    v_ref[...] = _dot(mix(2), wrkv_ref[2])
    w_pre = vec[0:1] + _dot(jnp.tanh(_dot(mix(3), w1_ref[...])), w2_ref[...])
    d_ref[...] = jnp.exp(-jnp.exp(-_softplus(-w_pre) - 0.5))
    a_ref[...] = _sigmoid(vec[1:2] + _dot(_dot(mix(4), a1_ref[...]), a2_ref[...]))
    g_ref[...] = _dot(_sigmoid(_dot(mix(5), g1_ref[...])), g2_ref[...])


def _rwkv_proj(x, shift, mu, w_rkv, w1, w2, a1, a2, g1, g2, vec, *, tm):
    B, T, D = x.shape
    M = B * T
    x2 = x.reshape(M, D)
    if shift is None:
        tm = min(tm, T)
        assert T % tm == 0
        halo = x2
        halo_spec = pl.BlockSpec((SUBLANES, D), lambda i: (jnp.maximum(i * (tm // SUBLANES) - 1, 0), 0))
    else:
        assert T == SUBLANES
        tm = min(tm, M)
        assert M % tm == 0
        halo = shift.reshape(B, 1, D)
        halo_spec = pl.BlockSpec((tm // T, 1, D), lambda i: (i, 0, 0))
    full = lambda arr: pl.BlockSpec(arr.shape, lambda i: (0,) * arr.ndim)
    row = pl.BlockSpec((tm, D), lambda i: (i, 0))
    outs = pl.pallas_call(
        functools.partial(_rwkv_proj_body, T, tm),
        grid=(M // tm,),
        in_specs=[row, halo_spec] + [full(w) for w in (mu, w_rkv, w1, w2, a1, a2, g1, g2, vec)],
        out_specs=[row] * 6,
        out_shape=[jax.ShapeDtypeStruct((M, D), F32)] * 6,
        compiler_params=_params(("arbitrary",)),
        name="rwkv_proj",
    )(x2, halo, mu, w_rkv, w1, w2, a1, a2, g1, g2, vec)
    return [o.reshape(B, T, D) for o in outs]


QUAD = 4
QUAD_LANES = QUAD * RWKV_HEAD
HALF_HEAD = RWKV_HEAD // 2
MXU_DEPTH = 256


def _fold_lanes(w):
    lead = w.shape[:-1]
    w = w.reshape(lead + (RWKV_HEADS // QUAD, QUAD, 2, HALF_HEAD))
    return jnp.swapaxes(w, -3, -2).reshape(lead + (RWKV_HEADS * RWKV_HEAD,))


def _split3(z):
    z1 = z.astype(BF16)
    r1 = z - z1.astype(F32)
    z2 = r1.astype(BF16)
    z3 = (r1 - z2.astype(F32)).astype(BF16)
    return z1, z2, z3


def _seg_sum_mxu(z, ones):
    n = z.shape[0]
    s = jnp.dot(jnp.concatenate(_split3(z), axis=0), ones, preferred_element_type=F32)
    return s[0:n] + s[n:2 * n] + s[2 * n:3 * n]


def _wkv_body(Gb, Gq, Tc, U, has_state, r_ref, k_ref, v_ref, d_ref, a_ref, vec_ref, *rest):
    if has_state:
        s0_ref, y_ref, sT_ref, S_scr, *bufs = rest
    else:
        y_ref, sT_ref, S_scr, *bufs = rest
    N, QL, HL = RWKV_HEAD, QUAD_LANES, LANES
    av_scr, bv_scr, dv_scr, rv_scr, bonus_scr, ycol_scr, vk_scr = [bufs[i::7] for i in range(7)]
    ic = pl.program_id(2)
    tiles = [(gb, gq) for gb in range(Gb) for gq in range(Gq)]
    G = len(tiles)
    n_parts = 3 * QUAD * U
    KP = LANES if n_parts <= LANES else MXU_DEPTH
    iota = lambda shape, axis: lax.broadcasted_iota(jnp.int32, shape, axis)
    head_folded = lambda lane: (lane % HL) // HALF_HEAD
    head_plain = lambda lane: lane // N
    lane_slot = iota((N, HL), 1) % HALF_HEAD
    lane_step = iota((N, KP), 1) % U
    ri, ci = iota((QL, QL), 0), iota((QL, QL), 1)
    block_ones = lambda same: jnp.where(same, 1.0, 0.0).astype(BF16)
    ones_ff = block_ones(head_folded(ri) == head_folded(ci))
    ones_fp = block_ones(head_folded(ri) == head_plain(ci))
    ones_pp = block_ones(head_plain(ri) == head_plain(ci))
    ones_step = block_ones((ri // HL == ci // HL) & (head_folded(ri) == head_folded(ci)))
    pad_rows = jnp.zeros((KP - n_parts, QL), F32)
    no_rows = jnp.zeros((N, HL), BF16)

    @pl.when(ic == 0)
    def _():
        for c, (gb, gq) in enumerate(tiles):
            if has_state:
                S_scr[c] = jnp.concatenate([s0_ref[gb, QUAD * gq + h][:, HALF_HEAD * half:HALF_HEAD * (half + 1)]
                                            for half in range(2) for h in range(QUAD)], axis=-1)
            else:
                S_scr[c] = jnp.zeros((N, QL), F32)

    def head_rows(z, head_of_lane):
        heads = head_of_lane(iota((U, QL), 1))
        return [jnp.where(heads == h, z, 0.0) for h in range(QUAD)]

    def half_sum(z):
        return (z[:, 0:HL] + z[:, HL:QL]).astype(BF16)

    def at(c, off):
        gb, gq = tiles[c]
        return (gb, pl.ds(off, U), slice(QL * gq, QL * (gq + 1)))

    def vec_of(c):
        return vec_ref[:, QL * tiles[c][1]:QL * (tiles[c][1] + 1)]

    def k_hat_of(c, off):
        return k_ref[at(c, off)] * (1.0 + (a_ref[at(c, off)] - 1.0) * vec_of(c)[1:2])

    def prepare_rows(off, slot):
        k_scaled, bonus_in = [], []
        for c in range(G):
            vec = vec_of(c)
            k_scaled.append(k_ref[at(c, off)] * vec[0:1])
            bonus_in.append(r_ref[at(c, off)] * k_hat_of(c, off) * vec[4:5])
        norms = _seg_sum_mxu(jnp.concatenate([z * z for z in k_scaled], axis=0), ones_ff)
        bonus = _seg_sum_mxu(jnp.concatenate(bonus_in, axis=0), ones_fp)
        for c in range(G):
            kk = k_scaled[c] / jnp.maximum(jnp.sqrt(norms[c * U:(c + 1) * U]), 1e-12)
            bonus_scr[slot][c] = bonus[c * U:(c + 1) * U] * v_ref[at(c, off)]
            av_scr[slot][c] = -kk
            bv_scr[slot][c] = kk * a_ref[at(c, off)]
            dv_scr[slot][c] = d_ref[at(c, off)]
            rv_scr[slot][c] = r_ref[at(c, off)]

    def prepare_outer(off, slot, c):
        v = v_ref[at(c, off)]
        k_hat = k_hat_of(c, off)
        v_hi = v.astype(BF16).astype(F32)
        v_lo = v - v_hi
        k_hi = k_hat.astype(BF16).astype(F32)
        k_lo = k_hat - k_hi
        v_stack = jnp.concatenate(head_rows(v_hi, head_plain) + head_rows(v_hi, head_plain)
                                  + head_rows(v_lo, head_plain) + [pad_rows], axis=0)
        v_t = v_stack.T
        v_cols = functools.reduce(jnp.add, [v_t[N * h:N * (h + 1)] for h in range(QUAD)]).astype(BF16)
        k_rows = jnp.concatenate(head_rows(k_hi, head_folded) + head_rows(k_lo, head_folded)
                                 + head_rows(k_hi, head_folded) + [pad_rows], axis=0).astype(BF16)
        lhs = jnp.concatenate([jnp.where(lane_step == t, v_cols, jnp.zeros_like(v_cols)) for t in range(U)], axis=0)
        vk_scr[slot][c] = jnp.dot(lhs, k_rows, preferred_element_type=F32)

    def preparation(off, slot):
        return [functools.partial(prepare_rows, off, slot)] + [functools.partial(prepare_outer, off, slot, c)
                                                               for c in range(G)]

    def recur(off, slot, side_work=()):
        side_at = {}
        for n, piece in enumerate(side_work):
            side_at.setdefault((n * U) // len(side_work), []).append(piece)
        S = [S_scr[c] for c in range(G)]
        y_cols = [jnp.zeros((N, HL), F32) for _ in tiles]
        for t in range(U + 1):
            blocks = []
            for c in range(G):
                s_bf = S[c].astype(BF16)
                sa_in = half_sum(s_bf * av_scr[slot][c, t:t + 1, :].astype(BF16)) if t < U else no_rows
                y_in = half_sum(s_bf * rv_scr[slot][c, t - 1:t, :].astype(BF16)) if t > 0 else no_rows
                blocks.append(jnp.concatenate([sa_in, y_in], axis=1))
            out = jnp.dot(jnp.concatenate(blocks, axis=0), ones_step, preferred_element_type=F32)
            for piece in side_at.get(t, ()):
                piece()
            for c in range(G):
                o = out[N * c:N * (c + 1)]
                if t > 0:
                    y_cols[c] = jnp.where(lane_slot == t - 1, o[:, HL:QL], y_cols[c])
                if t < U:
                    sa = jnp.concatenate([o[:, 0:HL], o[:, 0:HL]], axis=1)
                    S[c] = (S[c] * dv_scr[slot][c, t:t + 1, :] + sa * bv_scr[slot][c, t:t + 1, :]
                            + vk_scr[slot][c, N * t:N * (t + 1), :])
        for c in range(G):
            S_scr[c] = S[c]
            ycol_scr[slot][c] = y_cols[c]

    def emit(off, slot):
        ys = []
        for c in range(G):
            y_t = ycol_scr[slot][c].T
            ys.append(jnp.concatenate([y_t[HALF_HEAD * h:HALF_HEAD * h + U, :] for h in range(QUAD)], axis=-1))
        y_all = jnp.concatenate(ys, axis=0)
        yc_all = y_all - _seg_sum_mxu(y_all, ones_pp) * (1.0 / N)
        var_all = _seg_sum_mxu(yc_all * yc_all, ones_pp) * (1.0 / N)
        yn_all = yc_all * lax.rsqrt(var_all + GN_EPS)
        for c in range(G):
            vec = vec_of(c)
            y_ref[at(c, off)] = yn_all[c * U:(c + 1) * U] * vec[2:3] + vec[3:4] + bonus_scr[slot][c]

    n_sub = Tc // U
    for piece in preparation(0, 0):
        piece()
    if n_sub == 1:
        recur(0, 0)
        emit(0, 0)
    else:
        for c in range(G):
            ycol_scr[1][c] = jnp.zeros((N, HL), F32)
            bonus_scr[1][c] = jnp.zeros((U, QL), F32)

        def pair(jj, carry):
            off0 = pl.multiple_of(jj * (2 * U), 2 * U)
            off1 = pl.multiple_of(off0 + U, U)
            off2 = pl.multiple_of(jnp.minimum(off0 + 2 * U, Tc - U), U)
            before = pl.multiple_of(jnp.maximum(off0 - U, 0), U)
            recur(off0, 0, [functools.partial(emit, before, 1)] + preparation(off1, 1))
            recur(off1, 1, [functools.partial(emit, off0, 0)] + preparation(off2, 0))
            return carry

        lax.fori_loop(0, n_sub // 2, pair, 0)
        emit(Tc - U, 1)

    @pl.when(ic == pl.num_programs(2) - 1)
    def _():
        for c, (gb, gq) in enumerate(tiles):
            Sc = S_scr[c]
            for h in range(QUAD):
                sT_ref[gb, QUAD * gq + h] = jnp.concatenate(
                    [Sc[:, HALF_HEAD * h:HALF_HEAD * (h + 1)], Sc[:, HL + HALF_HEAD * h:HL + HALF_HEAD * (h + 1)]], axis=1)


def _wkv(r, k, v, d, a, vec8, state, *, Gb, Gq, Tc, U):
    B, T, D = r.shape
    H, N, QL = RWKV_HEADS, RWKV_HEAD, QUAD_LANES
    Tc = min(Tc, T)
    U = min(U, Tc)
    nq = H // QUAD
    assert 3 * QUAD * U <= MXU_DEPTH and U % SUBLANES == 0 and U <= HALF_HEAD
    assert T % Tc == 0 and Tc % U == 0 and (Tc == U or (Tc // U) % 2 == 0) and B % Gb == 0 and nq % Gq == 0
    G = Gb * Gq
    seq = pl.BlockSpec((Gb, Tc, QL * Gq), lambda b, p, c: (b, c, p))
    st = pl.BlockSpec((Gb, QUAD * Gq, N, N), lambda b, p, c: (b, p, 0, 0))
    operands = [r, k, v, d, a, vec8] + ([state] if state is not None else [])
    specs = [seq] * 5 + [pl.BlockSpec((8, QL * Gq), lambda b, p, c: (0, p))] + ([st] if state is not None else [])
    n_buf = 1 if Tc == U else 2
    rows = pltpu.VMEM((G, U, QL), F32)
    return pl.pallas_call(
        functools.partial(_wkv_body, Gb, Gq, Tc, U, state is not None),
        grid=(B // Gb, nq // Gq, T // Tc),
        in_specs=specs,
        out_specs=[seq, st],
        out_shape=[jax.ShapeDtypeStruct((B, T, D), F32), jax.ShapeDtypeStruct((B, H, N, N), F32)],
        scratch_shapes=[pltpu.VMEM((G, N, QL), F32)] + ([rows] * 5 + [pltpu.VMEM((G, N, LANES), F32),
                                                                      pltpu.VMEM((G, U * N, QL), F32)]) * n_buf,
        compiler_params=_params(("arbitrary", "arbitrary", "arbitrary")),
        name="wkv",
    )(*operands)


def _rwkv_layer(x, shift, wkv_state, prm, ln_g, ln_b, *, tm, wkv_cfg):
    mu, w_rkv, w1, w2, a1, a2, g1, g2, vec, r_k, w_o = prm
    B, T, D = x.shape
    w_rkv = jnp.stack([_fold_lanes(w_rkv[0]), _fold_lanes(w_rkv[1]), w_rkv[2]])
    vec_f = jnp.concatenate([_fold_lanes(vec[0:2]), vec[2:]], axis=0)
    r, k, v, d, a, g = _rwkv_proj(x, shift, mu, w_rkv, w1, _fold_lanes(w2), a1, _fold_lanes(a2), g1, g2, vec_f, tm=tm)
    vec8 = jnp.concatenate([_fold_lanes(vec[2:4]), vec[4:6], _fold_lanes(r_k.reshape(1, D)), jnp.zeros((3, D), F32)],
                           axis=0)
    y, s_new = _wkv(r, k, v, d, a, vec8, wkv_state, **wkv_cfg)
    x_new = _proj_norm(x.reshape(B * T, D), y.reshape(B * T, D), g.reshape(B * T, D), w_o,
                       jnp.zeros((D,), F32), ln_g, ln_b, tm=tm)
    return x_new.reshape(B, T, D), s_new, x[:, -1]


CONV_HIST = CONV_WIDTH - 1
CONV_HALO = 32


def _glu_body(x_ref, w_ref, b_ref, u_ref):
    h = _dot(x_ref[...], w_ref[...]) + b_ref[...]
    D = u_ref.shape[-1]
    u_ref[...] = h[:, :D] * _sigmoid(h[:, D:])


def _glu(x2, w, b, *, tm):
    M, D = x2.shape
    tm = min(tm, M)
    assert M % tm == 0
    return pl.pallas_call(
        _glu_body,
        grid=(M // tm,),
        in_specs=[pl.BlockSpec((tm, D), lambda i: (i, 0)),
                  pl.BlockSpec((D, 2 * D), lambda i: (0, 0)),
                  pl.BlockSpec((1, 2 * D), lambda i: (0, 0))],
        out_specs=pl.BlockSpec((tm, D), lambda i: (i, 0)),
        out_shape=jax.ShapeDtypeStruct((M, D), F32),
        compiler_params=_params(("arbitrary",)),
        name="conf_glu",
    )(x2, w, b.reshape(1, 2 * D))


def _conf_tail(conv, x, bdw_ref, cln_ref, w2_ref, b2_ref, g_ref, b_ref):
    cln = cln_ref[...]
    c = _silu(_layer_norm(conv + bdw_ref[...], cln[0:1], cln[1:2]))
    out = _dot(c, w2_ref[...]) + b2_ref[...]
    return _layer_norm(DN_ALPHA * x + out, g_ref[...], b_ref[...])


def _conf_long_body(seq_len, tm, x_ref, u_ref, halo_ref, wdw_ref, bdw_ref, cln_ref, w2_ref, b2_ref, g_ref, b_ref,
                    y_ref, ext_ref):
    i = pl.program_id(0)
    starts_sequence = (i * tm) % seq_len == 0
    ext_ref[0:CONV_HALO, :] = jnp.where(starts_sequence, 0.0, halo_ref[...])
    ext_ref[CONV_HALO:CONV_HALO + tm, :] = u_ref[...]
    wdw = wdw_ref[...]
    first = CONV_HALO - CONV_HIST
    conv = ext_ref[first:first + tm, :] * wdw[0:1]
    for j in range(1, CONV_WIDTH):
        conv = conv + ext_ref[first + j:first + j + tm, :] * wdw[j:j + 1]
    y_ref[...] = _conf_tail(conv, x_ref[...], bdw_ref, cln_ref, w2_ref, b2_ref, g_ref, b_ref)


def _conf_short_body(seq_len, nb, x_ref, u_ref, hist_ref, wdw_ref, bdw_ref, cln_ref, w2_ref, b2_ref, g_ref, b_ref,
                     y_ref, hist_out_ref, ext_ref):
    D = u_ref.shape[-1]
    ext_ref[:, 0:CONV_HIST, :] = hist_ref[...]
    ext_ref[:, CONV_HIST:CONV_HIST + seq_len, :] = u_ref[...].reshape(nb, seq_len, D)
    wdw = wdw_ref[...]
    conv = ext_ref[:, 0:seq_len, :] * wdw[0:1]
    for j in range(1, CONV_WIDTH):
        conv = conv + ext_ref[:, j:j + seq_len, :] * wdw[j:j + 1]
    hist_out_ref[...] = ext_ref[:, seq_len:seq_len + CONV_HIST, :]
    y_ref[...] = _conf_tail(conv.reshape(nb * seq_len, D), x_ref[...], bdw_ref, cln_ref, w2_ref, b2_ref, g_ref, b_ref)


def _conformer_layer(x, hist, prm, ln_g, ln_b, *, tm):
    w_pw1, b_pw1, w_dw, b_dw, cln, w_pw2, b_pw2 = prm
    B, T, D = x.shape
    M = B * T
    x2 = x.reshape(M, D)
    u = _glu(x2, w_pw1, b_pw1, tm=tm)
    const = lambda r, c: pl.BlockSpec((r, c), lambda i: (0, 0))
    weights = [w_dw, b_dw.reshape(1, D), cln, w_pw2, b_pw2.reshape(1, D), ln_g.reshape(1, D), ln_b.reshape(1, D)]
    wspecs = [const(CONV_WIDTH, D), const(1, D), const(2, D), const(D, D), const(1, D), const(1, D), const(1, D)]
    if hist is None:
        tm = min(tm, T)
        assert T % tm == 0 and tm % CONV_HALO == 0
        row = pl.BlockSpec((tm, D), lambda i: (i, 0))
        halo_spec = pl.BlockSpec((CONV_HALO, D), lambda i: (jnp.maximum(i * (tm // CONV_HALO) - 1, 0), 0))
        y = pl.pallas_call(
            functools.partial(_conf_long_body, T, tm),
            grid=(M // tm,),
            in_specs=[row, row, halo_spec] + wspecs,
            out_specs=row,
            out_shape=jax.ShapeDtypeStruct((M, D), F32),
            scratch_shapes=[pltpu.VMEM((CONV_HALO + tm, D), F32)],
            compiler_params=_params(("arbitrary",)),
            name="conf_conv_long",
        )(x2, u, u, *weights)
        new_hist = u.reshape(B, T, D)[:, T - CONV_HIST:, :]
    else:
        assert T == SUBLANES
        nb = min(tm // T, B)
        assert B % nb == 0
        row = pl.BlockSpec((nb * T, D), lambda i: (i, 0))
        hspec = pl.BlockSpec((nb, CONV_HIST, D), lambda i: (i, 0, 0))
        y, new_hist = pl.pallas_call(
            functools.partial(_conf_short_body, T, nb),
            grid=(B // nb,),
            in_specs=[row, row, hspec] + wspecs,
            out_specs=[row, hspec],
            out_shape=[jax.ShapeDtypeStruct((M, D), F32), jax.ShapeDtypeStruct((B, CONV_HIST, D), F32)],
            scratch_shapes=[pltpu.VMEM((nb, CONV_HIST + T + 2, D), F32)],
            compiler_params=_params(("arbitrary",)),
            name="conf_conv_short",
        )(x2, u, hist, *weights)
    return y.reshape(B, T, D), new_hist


def _rms_norm(z, g):
    return z * lax.rsqrt(jnp.mean(z * z, axis=-1, keepdims=True) + RMS_EPS) * g


def _mla_proj_body(keys_bf16, x_ref, cos_ref, sin_ref, wdq_ref, qn_ref, wqn_ref, wqr_ref, wqs_ref, wckv_ref, wkr_ref,
                   wks_ref, kvn_ref, wuk_ref, qlat_ref, qpe_ref, ckv_ref, kpe_ref, *key_copies):
    x = x_ref[...]
    cos, sin = cos_ref[...], sin_ref[...]
    cq = _rms_norm(_dot(x, wdq_ref[...]), qn_ref[...])
    q_nope = _dot(cq, wqn_ref[...])
    q_rope = _dot(cq, wqr_ref[...])
    q_swap = _dot(cq, wqs_ref[...])
    for h in range(MLA_HEADS):
        q_lat = _dot(q_nope[:, h * QK_NOPE:(h + 1) * QK_NOPE], wuk_ref[h])
        qlat_ref[h] = (q_lat * MLA_SCALE).astype(qlat_ref.dtype)
        sl = slice(h * QK_ROPE, (h + 1) * QK_ROPE)
        qpe_ref[h] = ((q_rope[:, sl] * cos + q_swap[:, sl] * sin) * MLA_SCALE).astype(qpe_ref.dtype)
    ckv = _rms_norm(_dot(x, wckv_ref[...]), kvn_ref[...])
    kpe = _dot(x, wkr_ref[...]) * cos + _dot(x, wks_ref[...]) * sin
    ckv_ref[...] = ckv
    kpe_ref[...] = kpe
    if keys_bf16:
        key_copies[0][...] = ckv.astype(BF16)
        key_copies[1][...] = kpe.astype(BF16)


def _swap_halves(w):
    half = w.shape[-1] // 2
    return jnp.concatenate([w[..., half:], w[..., :half]], axis=-1)


def _rope_tables(pos):
    half = QK_ROPE // 2
    inv = ROPE_THETA ** (-jnp.arange(half, dtype=F32) / half)
    ang = pos[:, None] * inv[None, :]
    cos, sin = jnp.cos(ang), jnp.sin(ang)
    return jnp.concatenate([cos, cos], -1), jnp.concatenate([-sin, sin], -1)


def _mla_proj(x2, pos, w_dq, q_norm, w_uq, w_dkv, kv_norm, w_uk, *, tm, keys_bf16):
    M, D = x2.shape
    H = MLA_HEADS
    tm = min(tm, M)
    assert M % tm == 0
    cos, sin = _rope_tables(pos)
    w_qn = w_uq[:, :, :QK_NOPE].reshape(Q_LORA, H * QK_NOPE)
    w_qr = w_uq[:, :, QK_NOPE:].reshape(Q_LORA, H * QK_ROPE)
    w_qs = _swap_halves(w_uq[:, :, QK_NOPE:]).reshape(Q_LORA, H * QK_ROPE)
    w_ckv, w_kr = w_dkv[:, :KV_LORA], w_dkv[:, KV_LORA:]
    w_ks = _swap_halves(w_kr)
    w_ukT = jnp.transpose(w_uk, (1, 2, 0))
    full = lambda arr: pl.BlockSpec(arr.shape, lambda i: (0,) * arr.ndim)
    row = lambda c: pl.BlockSpec((tm, c), lambda i: (i, 0))
    hrow = lambda c: pl.BlockSpec((H, tm, c), lambda i: (0, i, 0))
    weights = [w_dq, q_norm.reshape(1, Q_LORA), w_qn, w_qr, w_qs, w_ckv, w_kr, w_ks, kv_norm.reshape(1, KV_LORA), w_ukT]
    qdt = BF16 if keys_bf16 else F32
    out_specs = [hrow(KV_LORA), hrow(QK_ROPE), row(KV_LORA), row(QK_ROPE)]
    out_shape = [jax.ShapeDtypeStruct((H, M, KV_LORA), qdt), jax.ShapeDtypeStruct((H, M, QK_ROPE), qdt),
                 jax.ShapeDtypeStruct((M, KV_LORA), F32), jax.ShapeDtypeStruct((M, QK_ROPE), F32)]
    if keys_bf16:
        out_specs += [row(KV_LORA), row(QK_ROPE)]
        out_shape += [jax.ShapeDtypeStruct((M, KV_LORA), BF16), jax.ShapeDtypeStruct((M, QK_ROPE), BF16)]
    return pl.pallas_call(
        functools.partial(_mla_proj_body, keys_bf16),
        grid=(M // tm,),
        in_specs=[row(D), row(QK_ROPE), row(QK_ROPE)] + [full(w) for w in weights],
        out_specs=out_specs,
        out_shape=out_shape,
        compiler_params=_params(("arbitrary",)),
        name="mla_proj",
    )(x2, cos, sin, *weights)


def _scores(ql, qp, ckv, kpe):
    return _dot_t(ql, ckv) + _dot_t(qp, kpe)


def _softmax_step(s, values, m_ref, l_ref, acc_ref):
    m_old = m_ref[...]
    m_new = jnp.maximum(m_old, jnp.max(s, axis=-1, keepdims=True))
    p = jnp.exp(s - m_new)
    alpha = jnp.exp(m_old - m_new)
    l_ref[...] = alpha * l_ref[...] + jnp.sum(p, axis=-1, keepdims=True)
    pv = functools.reduce(jnp.add, [_dot(p[:, ks], v) for ks, v in values])
    acc_ref[...] = alpha * acc_ref[...] + pv
    m_ref[...] = m_new


HEAD_GROUP = 1


def _flash_body(tq, tk, qi_ref, kj_ref, qlat_ref, qpe_ref, ckv_ref, kpe_ref, o_ref, m_ref, l_ref, acc_ref):
    n = pl.program_id(1)
    i, j = qi_ref[n], kj_ref[n]
    H = MLA_HEADS
    last_j = ((i + 1) * tq - 1) // tk

    @pl.when(j == 0)
    def _():
        m_ref[...] = jnp.full(m_ref.shape, -jnp.inf, F32)
        l_ref[...] = jnp.zeros(l_ref.shape, F32)
        acc_ref[...] = jnp.zeros(acc_ref.shape, F32)

    def update(masked):
        ckv, kpe = ckv_ref[...], kpe_ref[...]

        def group_scores(h0):
            hs = slice(h0, h0 + HEAD_GROUP)
            return _scores(qlat_ref[hs].reshape(HEAD_GROUP * tq, KV_LORA), qpe_ref[hs].reshape(HEAD_GROUP * tq, QK_ROPE),
                           ckv, kpe)

        starts = list(range(0, H, HEAD_GROUP))
        s_next = group_scores(starts[0])
        for g, h0 in enumerate(starts):
            s = s_next
            if g + 1 < len(starts):
                s_next = group_scores(starts[g + 1])
            rows = slice(h0 * tq, (h0 + HEAD_GROUP) * tq)
            if masked:
                q_pos = i * tq + lax.broadcasted_iota(jnp.int32, (HEAD_GROUP * tq, 1), 0) % tq
                k_pos = j * tk + lax.broadcasted_iota(jnp.int32, (1, tk), 1)
                s = jnp.where(k_pos <= q_pos, s, -jnp.inf)
            _softmax_step(s, [(slice(None), ckv)], m_ref.at[rows], l_ref.at[rows], acc_ref.at[rows])

    @pl.when(j < last_j)
    def _():
        update(False)

    @pl.when(j == last_j)
    def _():
        update(True)
        o_ref[...] = (acc_ref[...] / l_ref[...]).reshape(H, tq, KV_LORA)


def _flash_attention(q_lat, q_pe, ckv, kpe, B, T, *, tq, tk):
    H = MLA_HEADS
    tq, tk = min(tq, T), min(tk, T)
    assert T % tq == 0 and T % tk == 0 and tk % tq == 0
    nq, nk = T // tq, T // tk
    pairs = [(i, j) for i in range(nq) for j in range(((i + 1) * tq - 1) // tk + 1)]
    qi = jnp.asarray([p[0] for p in pairs], jnp.int32)
    kj = jnp.asarray([p[1] for p in pairs], jnp.int32)
    qmap = lambda b, n, qi, kj: (0, b * nq + qi[n], 0)
    kmap = lambda b, n, qi, kj: (b * nk + kj[n], 0)
    grid_spec = pltpu.PrefetchScalarGridSpec(
        num_scalar_prefetch=2,
        grid=(B, len(pairs)),
        in_specs=[pl.BlockSpec((H, tq, KV_LORA), qmap), pl.BlockSpec((H, tq, QK_ROPE), qmap),
                  pl.BlockSpec((tk, KV_LORA), kmap), pl.BlockSpec((tk, QK_ROPE), kmap)],
        out_specs=pl.BlockSpec((H, tq, KV_LORA), qmap),
        scratch_shapes=[pltpu.VMEM((H * tq, 1), F32), pltpu.VMEM((H * tq, 1), F32), pltpu.VMEM((H * tq, KV_LORA), F32)],
    )
    return pl.pallas_call(
        functools.partial(_flash_body, tq, tk),
        grid_spec=grid_spec,
        out_shape=jax.ShapeDtypeStruct((H, B * T, KV_LORA), F32),
        compiler_params=_params(("arbitrary", "arbitrary")),
        name="mla_flash",
    )(qi, kj, q_lat, q_pe, ckv, kpe)


def _paged_body(PG, T, page, pt_ref, qlat_ref, qpe_ref, ckvn_ref, kpen_ref, *rest):
    pages_ckv, pages_kpe = rest[:PG], rest[PG:2 * PG]
    o_ref, m_ref, l_ref, acc_ref, selfk_ref, selfp_ref = rest[2 * PG:]
    g = pl.program_id(1)
    H = MLA_HEADS
    ql = qlat_ref[...].reshape(H * T, KV_LORA).astype(BF16)
    qp = qpe_ref[...].reshape(H * T, QK_ROPE).astype(BF16)

    @pl.when(g == 0)
    def _():
        m_ref[...] = jnp.full(m_ref.shape, -jnp.inf, F32)
        l_ref[...] = jnp.zeros(l_ref.shape, F32)
        acc_ref[...] = jnp.zeros(acc_ref.shape, F32)

    halves = [range(0, PG // 2), range(PG // 2, PG)] if PG > 1 else [range(PG)]
    keys = [jnp.concatenate([pages_ckv[p][0, 0].astype(BF16) for p in half], axis=0) for half in halves]
    rope_keys_t = [jnp.concatenate([pages_kpe[p][0, 0].astype(BF16) for p in half], axis=1) for half in halves]
    scores = [_dot_t(ql, k) + _dot(qp, kt) for k, kt in zip(keys, rope_keys_t)]
    for s, k in zip(scores, keys):
        _softmax_step(s, [(slice(None), k)], m_ref, l_ref, acc_ref)

    @pl.when(g == pl.num_programs(1) - 1)
    def _():
        selfk_ref[...] = jnp.zeros(selfk_ref.shape, F32)
        selfp_ref[...] = jnp.zeros(selfp_ref.shape, F32)
        selfk_ref[0:T, :] = ckvn_ref[...]
        selfp_ref[0:T, :] = kpen_ref[...]
        ckv = selfk_ref[...]
        s = _scores(ql, qp, ckv, selfp_ref[...])
        q_pos = lax.broadcasted_iota(jnp.int32, (H * T, 1), 0) % T
        k_pos = lax.broadcasted_iota(jnp.int32, (1, page), 1)
        s = jnp.where(k_pos <= q_pos, s, -jnp.inf)
        _softmax_step(s, [(slice(None), ckv)], m_ref, l_ref, acc_ref)
        o_ref[...] = (acc_ref[...] / l_ref[...]).reshape(H, T, KV_LORA)


def _paged_attention(q_lat, q_pe, ckv, kpe, cache_ckv, cache_kpe_t, page_table, layer, T, *, PG):
    H = MLA_HEADS
    DB, n_pages = page_table.shape
    page = cache_ckv.shape[2]
    assert n_pages % PG == 0 and T <= page
    qmap = lambda b, g, pt: (0, b, 0)
    nmap = lambda b, g, pt: (b, 0)
    pmap = lambda p: (lambda b, g, pt: (layer, pt[b * n_pages + g * PG + p], 0, 0))
    grid_spec = pltpu.PrefetchScalarGridSpec(
        num_scalar_prefetch=1,
        grid=(DB, n_pages // PG),
        in_specs=[pl.BlockSpec((H, T, KV_LORA), qmap), pl.BlockSpec((H, T, QK_ROPE), qmap),
                  pl.BlockSpec((T, KV_LORA), nmap), pl.BlockSpec((T, QK_ROPE), nmap)]
                 + [pl.BlockSpec((1, 1, page, KV_LORA), pmap(p)) for p in range(PG)]
                 + [pl.BlockSpec((1, 1, QK_ROPE, page), pmap(p)) for p in range(PG)],
        out_specs=pl.BlockSpec((H, T, KV_LORA), qmap),
        scratch_shapes=[pltpu.VMEM((H * T, 1), F32), pltpu.VMEM((H * T, 1), F32), pltpu.VMEM((H * T, KV_LORA), F32),
                        pltpu.VMEM((page, KV_LORA), F32), pltpu.VMEM((page, QK_ROPE), F32)],
    )
    return pl.pallas_call(
        functools.partial(_paged_body, PG, T, page),
        grid_spec=grid_spec,
        out_shape=jax.ShapeDtypeStruct((H, DB * T, KV_LORA), F32),
        compiler_params=_params(("arbitrary", "arbitrary")),
        name="mla_paged",
    )(page_table.reshape(-1), q_lat, q_pe, ckv, kpe, *([cache_ckv] * PG), *([cache_kpe_t] * PG))


def _mla_out_body(x_ref, o_ref, wuv_ref, wo_ref, g_ref, b_ref, y_ref):
    heads = [_dot(o_ref[h], wuv_ref[h]) for h in range(MLA_HEADS)]
    out = _dot(jnp.concatenate(heads, axis=-1), wo_ref[...])
    y_ref[...] = _layer_norm(DN_ALPHA * x_ref[...] + out, g_ref[...], b_ref[...])


def _mla_out(x2, o_lat, w_uv, w_o, ln_g, ln_b, *, tm):
    M, D = x2.shape
    H = MLA_HEADS
    tm = min(tm, M)
    assert M % tm == 0
    w_uvh = jnp.transpose(w_uv, (1, 0, 2))
    const = lambda shape: pl.BlockSpec(shape, lambda i: (0,) * len(shape))
    return pl.pallas_call(
        _mla_out_body,
        grid=(M // tm,),
        in_specs=[pl.BlockSpec((tm, D), lambda i: (i, 0)), pl.BlockSpec((H, tm, KV_LORA), lambda i: (0, i, 0)),
                  const(w_uvh.shape), const(w_o.shape), const((1, D)), const((1, D))],
        out_specs=pl.BlockSpec((tm, D), lambda i: (i, 0)),
        out_shape=jax.ShapeDtypeStruct((M, D), F32),
        compiler_params=_params(("arbitrary",)),
        name="mla_out",
    )(x2, o_lat, w_uvh, w_o, ln_g.reshape(1, D), ln_b.reshape(1, D))


def _mla_layer(x, pos0, cache, prm, ln_g, ln_b, *, tm, attn_cfg):
    w_dq, q_norm, w_uq, w_dkv, kv_norm, w_uk, w_uv, w_o = prm
    B, T, D = x.shape
    x2 = x.reshape(B * T, D)
    pos = jnp.tile(pos0 + jnp.arange(T, dtype=F32), B)
    proj = _mla_proj(x2, pos, w_dq, q_norm, w_uq, w_dkv, kv_norm, w_uk, tm=tm, keys_bf16=cache is None)
    q_lat, q_pe, ckv, kpe = proj[:4]
    if cache is None:
        o_lat = _flash_attention(q_lat, q_pe, proj[4], proj[5], B, T, **attn_cfg)
    else:
        o_lat = _paged_attention(q_lat, q_pe, ckv, kpe, *cache, T, **attn_cfg)
    y = _mla_out(x2, o_lat, w_uv, w_o, ln_g, ln_b, tm=tm)
    return y.reshape(B, T, D), ckv.reshape(B, T, KV_LORA), kpe.reshape(B, T, QK_ROPE)


def kernel(x_prompt, x_sample, state_wkv, state_shift, state_conv, cache_mla_ckv, cache_mla_kpe, state_ffn_conv, page_table, ln_g, ln_b, rwkv_mu, rwkv_w_rkv, rwkv_w1, rwkv_w2, rwkv_a1, rwkv_a2, rwkv_g1, rwkv_g2, rwkv_vec, rwkv_r_k, rwkv_w_o, conf_w_pw1, conf_b_pw1, conf_w_dw, conf_b_dw, conf_ln, conf_w_pw2, conf_b_pw2, mla_w_dq, mla_q_norm, mla_w_uq, mla_w_dkv, mla_kv_norm, mla_w_uk, mla_w_uv, mla_w_o, ffn_w_in, ffn_w_dw, ffn_b_dw, ffn_w_out):
    bf = lambda w: w.astype(BF16)
    past_len = page_table.shape[1] * cache_mla_ckv.shape[2]
    xp, xs = x_prompt, x_sample
    out_p = dict(wkv=[], shift=[], conv=[], ckv=[], kpe=[], ffn=[])
    out_s = dict(wkv=[], shift=[], conv=[], ckv=[], kpe=[], ffn=[])
    for i in range(DEPTH):
        j, kind = i // N_MIXERS, i % N_MIXERS
        g0, b0, g1, b1 = ln_g[i, 0], ln_b[i, 0], ln_g[i, 1], ln_b[i, 1]
        if kind == 0:
            prm = (rwkv_mu[j], bf(rwkv_w_rkv[j]), bf(rwkv_w1[j]), bf(rwkv_w2[j]), bf(rwkv_a1[j]), bf(rwkv_a2[j]),
                   bf(rwkv_g1[j]), bf(rwkv_g2[j]), rwkv_vec[j], rwkv_r_k[j], bf(rwkv_w_o[j]))
            xp, st, sh = _rwkv_layer(xp, None, None, prm, g0, b0, tm=512, wkv_cfg=dict(Gb=2, Gq=4, Tc=256, U=16))
            out_p["wkv"].append(st)
            out_p["shift"].append(sh)
            xs, st, sh = _rwkv_layer(xs, state_shift[j], state_wkv[j], prm, g0, b0, tm=512,
                                     wkv_cfg=dict(Gb=2, Gq=4, Tc=8, U=8))
            out_s["wkv"].append(st)
            out_s["shift"].append(sh)
        elif kind == 1:
            prm = (bf(conf_w_pw1[j]), conf_b_pw1[j], conf_w_dw[j], conf_b_dw[j], conf_ln[j], bf(conf_w_pw2[j]),
                   conf_b_pw2[j])
            xp, cb = _conformer_layer(xp, None, prm, g0, b0, tm=256)
            out_p["conv"].append(cb)
            xs, cb = _conformer_layer(xs, state_conv[j], prm, g0, b0, tm=256)
            out_s["conv"].append(cb)
        else:
            prm = (bf(mla_w_dq[j]), mla_q_norm[j], bf(mla_w_uq[j]), bf(mla_w_dkv[j]), mla_kv_norm[j],
                   bf(mla_w_uk[j]), bf(mla_w_uv[j]), bf(mla_w_o[j]))
            xp, ckv, kpe = _mla_layer(xp, 0.0, None, prm, g0, b0, tm=512, attn_cfg=dict(tq=256, tk=1024))
            out_p["ckv"].append(ckv)
            out_p["kpe"].append(kpe)
            xs, ckv, kpe = _mla_layer(xs, float(past_len), (cache_mla_ckv, jnp.swapaxes(cache_mla_kpe, 2, 3), page_table, j), prm,
                                      g0, b0, tm=512, attn_cfg=dict(PG=32))
            out_s["ckv"].append(ckv)
            out_s["kpe"].append(kpe)
        w_in, w_out = bf(ffn_w_in[i]), bf(ffn_w_out[i])
        xp, fb = _conv_ffn(xp, None, w_in, ffn_w_dw[i], ffn_b_dw[i], w_out, g1, b1, tm=512, fc=1408)
        out_p["ffn"].append(fb)
        xs, fb = _conv_ffn(xs, state_ffn_conv[i], w_in, ffn_w_dw[i], ffn_b_dw[i], w_out, g1, b1, tm=512, fc=1408)
        out_s["ffn"].append(fb)
    names = ("wkv", "shift", "conv", "ckv", "kpe", "ffn")
    return (xp, xs) + tuple(jnp.stack(out_p[n]) for n in names) + tuple(jnp.stack(out_s[n]) for n in names)
```

```python
import functools

import jax
import jax.numpy as jnp
from jax import lax
from jax.experimental import pallas as pl
from jax.experimental.pallas import tpu as pltpu

D_MODEL = 1024
DEPTH = 4
N_MIXERS = 3
DN_ALPHA = (2 * DEPTH) ** 0.25
LN_EPS = 1e-5
RWKV_HEAD = 64
RWKV_HEADS = D_MODEL // RWKV_HEAD
GN_EPS = 64e-5
CONV_WIDTH = 31
MLA_HEADS = 8
QK_NOPE = 128
QK_ROPE = 64
V_HEAD = 128
KV_LORA = 256
Q_LORA = 384
ROPE_THETA = 10000.0
MLA_SCALE = (QK_NOPE + QK_ROPE) ** -0.5
RMS_EPS = 1e-6
D_FF = 2816
FFN_CONV_WIDTH = 3

SUBLANES = 8
LANES = 128
VMEM_LIMIT = 56 * 1024 * 1024

BF16 = jnp.bfloat16
F32 = jnp.float32


def _params(sem):
    return pltpu.CompilerParams(dimension_semantics=sem, vmem_limit_bytes=VMEM_LIMIT)


def _dot(a, b):
    return jnp.dot(a.astype(BF16), b.astype(BF16), preferred_element_type=F32)


def _dot_t(a, b):
    return lax.dot_general(a.astype(BF16), b.astype(BF16), (((1,), (1,)), ((), ())),
                           preferred_element_type=F32)


def _layer_norm(z, g, b, eps=LN_EPS):
    mu = jnp.mean(z, axis=-1, keepdims=True)
    zc = z - mu
    var = jnp.mean(zc * zc, axis=-1, keepdims=True)
    return zc * lax.rsqrt(var + eps) * g + b


def _sigmoid(z):
    return 1.0 / (1.0 + jnp.exp(-z))


def _silu(z):
    return z * _sigmoid(z)


def _prev_rows(cur, shift, pos, fill):
    out = pltpu.roll(cur, shift, 0)
    for p in range(shift):
        out = jnp.where(pos == p, fill[shift - 1 - p], out)
    return out


def _ffn_body(seq_len, tm, x_ref, halo_ref, wa_ref, wb_ref, wdw_ref, bdw_ref, wo_ref, g_ref, b_ref,
              y_ref, tail_ref, acc_ref, xb_ref):
    i, f = pl.program_id(0), pl.program_id(1)

    @pl.when(f == 0)
    def _():
        xb_ref[...] = x_ref[...].astype(BF16)

    xb = xb_ref[...]
    a = _dot(xb, wa_ref[...])
    b = _dot(xb, wb_ref[...])
    fc = a.shape[-1]
    rows = lax.broadcasted_iota(jnp.int32, (tm, 1), 0)
    if seq_len >= tm:
        starts_sequence = (i * tm) % seq_len == 0
        ah = _dot(halo_ref[...], wa_ref[...])
        ah = jnp.where(starts_sequence, 0.0, ah)
        pos = rows
        before1, before2 = ah[SUBLANES - 1:SUBLANES], ah[SUBLANES - 2:SUBLANES - 1]
    else:
        hist = halo_ref[...]
        nseq = tm // seq_len
        expand = lambda r: jnp.broadcast_to(r, (nseq, seq_len, fc)).reshape(tm, fc)
        pos = rows % seq_len
        before1, before2 = expand(hist[:, 1:2, :]), expand(hist[:, 0:1, :])
    a1 = _prev_rows(a, 1, pos, [before1])
    a2 = _prev_rows(a, 2, pos, [before1, before2])
    wdw = wdw_ref[...]
    c = a2 * wdw[0:1] + a1 * wdw[1:2] + a * wdw[2:3] + bdw_ref[...]
    part = _dot(_silu(c) * b, wo_ref[...])

    @pl.when(f == 0)
    def _():
        acc_ref[...] = part

    @pl.when(f > 0)
    def _():
        acc_ref[...] += part

    if seq_len >= tm:
        tail_ref[0] = a[tm - SUBLANES:tm]
    else:
        tail_ref[...] = a.reshape(tm // seq_len, seq_len, fc)

    @pl.when(f == pl.num_programs(1) - 1)
    def _():
        y_ref[...] = _layer_norm(DN_ALPHA * x_ref[...] + acc_ref[...], g_ref[...], b_ref[...])


def _conv_ffn(x, hist, w_in, w_dw, b_dw, w_out, ln_g, ln_b, *, tm, fc):
    B, T, D = x.shape
    F = w_out.shape[0]
    M = B * T
    x2 = x.reshape(M, D)
    nF = F // fc
    long_seq = hist is None
    if long_seq:
        tm = min(tm, T)
        assert T % tm == 0 and tm % SUBLANES == 0
        halo = x2
        halo_spec = pl.BlockSpec((SUBLANES, D), lambda i, f: (jnp.maximum(i * (tm // SUBLANES) - 1, 0), 0))
        tail_shape = jax.ShapeDtypeStruct((M // tm, SUBLANES, F), F32)
        tail_spec = pl.BlockSpec((1, SUBLANES, fc), lambda i, f: (i, 0, f))
    else:
        assert T == SUBLANES
        tm = min(tm, M)
        assert M % tm == 0 and tm % T == 0
        halo = hist
        halo_spec = pl.BlockSpec((tm // T, FFN_CONV_WIDTH - 1, fc), lambda i, f: (i, 0, f))
        tail_shape = jax.ShapeDtypeStruct((B, T, F), F32)
        tail_spec = pl.BlockSpec((tm // T, T, fc), lambda i, f: (i, 0, f))
    y, tail = pl.pallas_call(
        functools.partial(_ffn_body, T, tm),
        grid=(M // tm, nF),
        in_specs=[
            pl.BlockSpec((tm, D), lambda i, f: (i, 0)),
            halo_spec,
            pl.BlockSpec((D, fc), lambda i, f: (0, f)),
            pl.BlockSpec((D, fc), lambda i, f: (0, nF + f)),
            pl.BlockSpec((FFN_CONV_WIDTH, fc), lambda i, f: (0, f)),
            pl.BlockSpec((1, fc), lambda i, f: (0, f)),
            pl.BlockSpec((fc, D), lambda i, f: (f, 0)),
            pl.BlockSpec((1, D), lambda i, f: (0, 0)),
            pl.BlockSpec((1, D), lambda i, f: (0, 0)),
        ],
        out_specs=[pl.BlockSpec((tm, D), lambda i, f: (i, 0)), tail_spec],
        out_shape=[jax.ShapeDtypeStruct((M, D), F32), tail_shape],
        scratch_shapes=[pltpu.VMEM((tm, D), F32), pltpu.VMEM((tm, D), BF16)],
        compiler_params=_params(("arbitrary", "arbitrary")),
        name="conv_ffn",
    )(x2, halo, w_in, w_in, w_dw, b_dw.reshape(1, F), w_out, ln_g.reshape(1, D), ln_b.reshape(1, D))
    if long_seq:
        tail = tail[T // tm - 1::T // tm]
    return y.reshape(B, T, D), tail[:, -(FFN_CONV_WIDTH - 1):, :]


def _proj_norm_body(gated, *refs):
    if gated:
        x_ref, h_ref, gate_ref, w_ref, bias_ref, g_ref, b_ref, y_ref = refs
        h = h_ref[...] * gate_ref[...]
    else:
        x_ref, h_ref, w_ref, bias_ref, g_ref, b_ref, y_ref = refs
        h = h_ref[...]
    out = _dot(h, w_ref[...]) + bias_ref[...]
    y_ref[...] = _layer_norm(DN_ALPHA * x_ref[...] + out, g_ref[...], b_ref[...])


def _proj_norm(x, h, gate, w, bias, ln_g, ln_b, *, tm):
    M, D = x.shape
    K = h.shape[1]
    tm = min(tm, M)
    assert M % tm == 0
    row = lambda c: pl.BlockSpec((tm, c), lambda i: (i, 0))
    const = lambda r, c: pl.BlockSpec((r, c), lambda i: (0, 0))
    operands = [x, h] + ([gate] if gate is not None else []) + [w, bias.reshape(1, D), ln_g.reshape(1, D), ln_b.reshape(1, D)]
    specs = [row(D), row(K)] + ([row(K)] if gate is not None else []) + [const(K, D), const(1, D), const(1, D), const(1, D)]
    return pl.pallas_call(
        functools.partial(_proj_norm_body, gate is not None),
        grid=(M // tm,),
        in_specs=specs,
        out_specs=row(D),
        out_shape=jax.ShapeDtypeStruct((M, D), F32),
        compiler_params=_params(("arbitrary",)),
        name="proj_norm",
    )(*operands)


def _softplus(z):
    return jnp.maximum(z, 0.0) + jnp.log(1.0 + jnp.exp(-jnp.abs(z)))


def _rwkv_proj_body(seq_len, tm, x_ref, halo_ref, mu_ref, wrkv_ref, w1_ref, w2_ref, a1_ref, a2_ref,
                    g1_ref, g2_ref, vec_ref, r_ref, k_ref, v_ref, d_ref, a_ref, g_ref):
    i = pl.program_id(0)
    x = x_ref[...]
    D = x.shape[-1]
    rows = lax.broadcasted_iota(jnp.int32, (tm, 1), 0)
    if seq_len >= tm:
        starts_sequence = (i * tm) % seq_len == 0
        before = jnp.where(starts_sequence, 0.0, halo_ref[SUBLANES - 1:SUBLANES, :])
        pos = rows
    else:
        nseq = tm // seq_len
        before = jnp.broadcast_to(halo_ref[...], (nseq, seq_len, D)).reshape(tm, D)
        pos = rows % seq_len
    dx = _prev_rows(x, 1, pos, [before]) - x
    mu = mu_ref[...]
    mix = lambda j: x + dx * mu[j:j + 1]
    vec = vec_ref[...]
    r_ref[...] = _dot(mix(0), wrkv_ref[0])
    k_ref[...] = _dot(mix(1), wrkv_ref[1])
    v_ref[...] = _dot(mix(2), wrkv_ref[2])
    w_pre = vec[0:1] + _dot(jnp.tanh(_dot(mix(3), w1_ref[...])), w2_ref[...])
    d_ref[...] = jnp.exp(-jnp.exp(-_softplus(-w_pre) - 0.5))
    a_ref[...] = _sigmoid(vec[1:2] + _dot(_dot(mix(4), a1_ref[...]), a2_ref[...]))
    g_ref[...] = _dot(_sigmoid(_dot(mix(5), g1_ref[...])), g2_ref[...])


def _rwkv_proj(x, shift, mu, w_rkv, w1, w2, a1, a2, g1, g2, vec, *, tm):
    B, T, D = x.shape
    M = B * T
    x2 = x.reshape(M, D)
    if shift is None:
        tm = min(tm, T)
        assert T % tm == 0
        halo = x2
        halo_spec = pl.BlockSpec((SUBLANES, D), lambda i: (jnp.maximum(i * (tm // SUBLANES) - 1, 0), 0))
    else:
        assert T == SUBLANES
        tm = min(tm, M)
        assert M % tm == 0
        halo = shift.reshape(B, 1, D)
        halo_spec = pl.BlockSpec((tm // T, 1, D), lambda i: (i, 0, 0))
    full = lambda arr: pl.BlockSpec(arr.shape, lambda i: (0,) * arr.ndim)
    row = pl.BlockSpec((tm, D), lambda i: (i, 0))
    outs = pl.pallas_call(
        functools.partial(_rwkv_proj_body, T, tm),
        grid=(M // tm,),
        in_specs=[row, halo_spec] + [full(w) for w in (mu, w_rkv, w1, w2, a1, a2, g1, g2, vec)],
        out_specs=[row] * 6,
        out_shape=[jax.ShapeDtypeStruct((M, D), F32)] * 6,
        compiler_params=_params(("arbitrary",)),
        name="rwkv_proj",
    )(x2, halo, mu, w_rkv, w1, w2, a1, a2, g1, g2, vec)
    return [o.reshape(B, T, D) for o in outs]


QUAD = 4
QUAD_LANES = QUAD * RWKV_HEAD
HALF_HEAD = RWKV_HEAD // 2
MXU_DEPTH = 256


def _fold_lanes(w):
    lead = w.shape[:-1]
    w = w.reshape(lead + (RWKV_HEADS // QUAD, QUAD, 2, HALF_HEAD))
    return jnp.swapaxes(w, -3, -2).reshape(lead + (RWKV_HEADS * RWKV_HEAD,))


def _split3(z):
    z1 = z.astype(BF16)
    r1 = z - z1.astype(F32)
    z2 = r1.astype(BF16)
    z3 = (r1 - z2.astype(F32)).astype(BF16)
    return z1, z2, z3


def _seg_sum_mxu(z, ones):
    n = z.shape[0]
    s = jnp.dot(jnp.concatenate(_split3(z), axis=0), ones, preferred_element_type=F32)
    return s[0:n] + s[n:2 * n] + s[2 * n:3 * n]


def _wkv_body(Gb, Gq, Tc, U, has_state, r_ref, k_ref, v_ref, d_ref, a_ref, vec_ref, *rest):
    if has_state:
        s0_ref, y_ref, sT_ref, S_scr, *bufs = rest
    else:
        y_ref, sT_ref, S_scr, *bufs = rest
    N, QL, HL = RWKV_HEAD, QUAD_LANES, LANES
    av_scr, bv_scr, dv_scr, rv_scr, bonus_scr, ycol_scr, vk_scr = [bufs[i::7] for i in range(7)]
    ic = pl.program_id(2)
    tiles = [(gb, gq) for gb in range(Gb) for gq in range(Gq)]
    G = len(tiles)
    n_parts = 3 * QUAD * U
    KP = LANES if n_parts <= LANES else MXU_DEPTH
    iota = lambda shape, axis: lax.broadcasted_iota(jnp.int32, shape, axis)
    head_folded = lambda lane: (lane % HL) // HALF_HEAD
    head_plain = lambda lane: lane // N
    lane_slot = iota((N, HL), 1) % HALF_HEAD
    lane_step = iota((N, KP), 1) % U
    ri, ci = iota((QL, QL), 0), iota((QL, QL), 1)
    block_ones = lambda same: jnp.where(same, 1.0, 0.0).astype(BF16)
    ones_ff = block_ones(head_folded(ri) == head_folded(ci))
    ones_fp = block_ones(head_folded(ri) == head_plain(ci))
    ones_pp = block_ones(head_plain(ri) == head_plain(ci))
    ones_step = block_ones((ri // HL == ci // HL) & (head_folded(ri) == head_folded(ci)))
    pad_rows = jnp.zeros((KP - n_parts, QL), F32)
    no_rows = jnp.zeros((N, HL), BF16)

    @pl.when(ic == 0)
    def _():
        for c, (gb, gq) in enumerate(tiles):
            if has_state:
                S_scr[c] = jnp.concatenate([s0_ref[gb, QUAD * gq + h][:, HALF_HEAD * half:HALF_HEAD * (half + 1)]
                                            for half in range(2) for h in range(QUAD)], axis=-1)
            else:
                S_scr[c] = jnp.zeros((N, QL), F32)

    def head_rows(z, head_of_lane):
        heads = head_of_lane(iota((U, QL), 1))
        return [jnp.where(heads == h, z, 0.0) for h in range(QUAD)]

    def half_sum(z):
        return (z[:, 0:HL] + z[:, HL:QL]).astype(BF16)

    def at(c, off):
        gb, gq = tiles[c]
        return (gb, pl.ds(off, U), slice(QL * gq, QL * (gq + 1)))

    def vec_of(c):
        return vec_ref[:, QL * tiles[c][1]:QL * (tiles[c][1] + 1)]

    def k_hat_of(c, off):
        return k_ref[at(c, off)] * (1.0 + (a_ref[at(c, off)] - 1.0) * vec_of(c)[1:2])

    def prepare_rows(off, slot):
        k_scaled, bonus_in = [], []
        for c in range(G):
            vec = vec_of(c)
            k_scaled.append(k_ref[at(c, off)] * vec[0:1])
            bonus_in.append(r_ref[at(c, off)] * k_hat_of(c, off) * vec[4:5])
        norms = _seg_sum_mxu(jnp.concatenate([z * z for z in k_scaled], axis=0), ones_ff)
        bonus = _seg_sum_mxu(jnp.concatenate(bonus_in, axis=0), ones_fp)
        for c in range(G):
            kk = k_scaled[c] / jnp.maximum(jnp.sqrt(norms[c * U:(c + 1) * U]), 1e-12)
            bonus_scr[slot][c] = bonus[c * U:(c + 1) * U] * v_ref[at(c, off)]
            av_scr[slot][c] = -kk
            bv_scr[slot][c] = kk * a_ref[at(c, off)]
            dv_scr[slot][c] = d_ref[at(c, off)]
            rv_scr[slot][c] = r_ref[at(c, off)]

    def prepare_outer(off, slot, c):
        v = v_ref[at(c, off)]
        k_hat = k_hat_of(c, off)
        v_hi = v.astype(BF16).astype(F32)
        v_lo = v - v_hi
        k_hi = k_hat.astype(BF16).astype(F32)
        k_lo = k_hat - k_hi
        v_stack = jnp.concatenate(head_rows(v_hi, head_plain) + head_rows(v_hi, head_plain)
                                  + head_rows(v_lo, head_plain) + [pad_rows], axis=0)
        v_t = v_stack.T
        v_cols = functools.reduce(jnp.add, [v_t[N * h:N * (h + 1)] for h in range(QUAD)]).astype(BF16)
        k_rows = jnp.concatenate(head_rows(k_hi, head_folded) + head_rows(k_lo, head_folded)
                                 + head_rows(k_hi, head_folded) + [pad_rows], axis=0).astype(BF16)
        lhs = jnp.concatenate([jnp.where(lane_step == t, v_cols, jnp.zeros_like(v_cols)) for t in range(U)], axis=0)
        vk_scr[slot][c] = jnp.dot(lhs, k_rows, preferred_element_type=F32)

    def preparation(off, slot):
        return [functools.partial(prepare_rows, off, slot)] + [functools.partial(prepare_outer, off, slot, c)
                                                               for c in range(G)]

    def recur(off, slot, side_work=()):
        side_at = {}
        for n, piece in enumerate(side_work):
            side_at.setdefault((n * U) // len(side_work), []).append(piece)
        S = [S_scr[c] for c in range(G)]
        y_cols = [jnp.zeros((N, HL), F32) for _ in tiles]
        for t in range(U + 1):
            blocks = []
            for c in range(G):
                s_bf = S[c].astype(BF16)
                sa_in = half_sum(s_bf * av_scr[slot][c, t:t + 1, :].astype(BF16)) if t < U else no_rows
                y_in = half_sum(s_bf * rv_scr[slot][c, t - 1:t, :].astype(BF16)) if t > 0 else no_rows
                blocks.append(jnp.concatenate([sa_in, y_in], axis=1))
            out = jnp.dot(jnp.concatenate(blocks, axis=0), ones_step, preferred_element_type=F32)
            for piece in side_at.get(t, ()):
                piece()
            for c in range(G):
                o = out[N * c:N * (c + 1)]
                if t > 0:
                    y_cols[c] = jnp.where(lane_slot == t - 1, o[:, HL:QL], y_cols[c])
                if t < U:
                    sa = jnp.concatenate([o[:, 0:HL], o[:, 0:HL]], axis=1)
                    S[c] = (S[c] * dv_scr[slot][c, t:t + 1, :] + sa * bv_scr[slot][c, t:t + 1, :]
                            + vk_scr[slot][c, N * t:N * (t + 1), :])
        for c in range(G):
            S_scr[c] = S[c]
            ycol_scr[slot][c] = y_cols[c]

    def emit(off, slot):
        ys = []
        for c in range(G):
            y_t = ycol_scr[slot][c].T
            ys.append(jnp.concatenate([y_t[HALF_HEAD * h:HALF_HEAD * h + U, :] for h in range(QUAD)], axis=-1))
        y_all = jnp.concatenate(ys, axis=0)
        yc_all = y_all - _seg_sum_mxu(y_all, ones_pp) * (1.0 / N)
        var_all = _seg_sum_mxu(yc_all * yc_all, ones_pp) * (1.0 / N)
        yn_all = yc_all * lax.rsqrt(var_all + GN_EPS)
        for c in range(G):
            vec = vec_of(c)
            y_ref[at(c, off)] = yn_all[c * U:(c + 1) * U] * vec[2:3] + vec[3:4] + bonus_scr[slot][c]

    n_sub = Tc // U
    for piece in preparation(0, 0):
        piece()
    if n_sub == 1:
        recur(0, 0)
        emit(0, 0)
    else:
        for c in range(G):
            ycol_scr[1][c] = jnp.zeros((N, HL), F32)
            bonus_scr[1][c] = jnp.zeros((U, QL), F32)

        def pair(jj, carry):
            off0 = pl.multiple_of(jj * (2 * U), 2 * U)
            off1 = pl.multiple_of(off0 + U, U)
            off2 = pl.multiple_of(jnp.minimum(off0 + 2 * U, Tc - U), U)
            before = pl.multiple_of(jnp.maximum(off0 - U, 0), U)
            recur(off0, 0, [functools.partial(emit, before, 1)] + preparation(off1, 1))
            recur(off1, 1, [functools.partial(emit, off0, 0)] + preparation(off2, 0))
            return carry

        lax.fori_loop(0, n_sub // 2, pair, 0)
        emit(Tc - U, 1)

    @pl.when(ic == pl.num_programs(2) - 1)
    def _():
        for c, (gb, gq) in enumerate(tiles):
            Sc = S_scr[c]
            for h in range(QUAD):
                sT_ref[gb, QUAD * gq + h] = jnp.concatenate(
                    [Sc[:, HALF_HEAD * h:HALF_HEAD * (h + 1)], Sc[:, HL + HALF_HEAD * h:HL + HALF_HEAD * (h + 1)]], axis=1)


def _wkv(r, k, v, d, a, vec8, state, *, Gb, Gq, Tc, U):
    B, T, D = r.shape
    H, N, QL = RWKV_HEADS, RWKV_HEAD, QUAD_LANES
    Tc = min(Tc, T)
    U = min(U, Tc)
    nq = H // QUAD
    assert 3 * QUAD * U <= MXU_DEPTH and U % SUBLANES == 0 and U <= HALF_HEAD
    assert T % Tc == 0 and Tc % U == 0 and (Tc == U or (Tc // U) % 2 == 0) and B % Gb == 0 and nq % Gq == 0
    G = Gb * Gq
    seq = pl.BlockSpec((Gb, Tc, QL * Gq), lambda b, p, c: (b, c, p))
    st = pl.BlockSpec((Gb, QUAD * Gq, N, N), lambda b, p, c: (b, p, 0, 0))
    operands = [r, k, v, d, a, vec8] + ([state] if state is not None else [])
    specs = [seq] * 5 + [pl.BlockSpec((8, QL * Gq), lambda b, p, c: (0, p))] + ([st] if state is not None else [])
    n_buf = 1 if Tc == U else 2
    rows = pltpu.VMEM((G, U, QL), F32)
    return pl.pallas_call(
        functools.partial(_wkv_body, Gb, Gq, Tc, U, state is not None),
        grid=(B // Gb, nq // Gq, T // Tc),
        in_specs=specs,
        out_specs=[seq, st],
        out_shape=[jax.ShapeDtypeStruct((B, T, D), F32), jax.ShapeDtypeStruct((B, H, N, N), F32)],
        scratch_shapes=[pltpu.VMEM((G, N, QL), F32)] + ([rows] * 5 + [pltpu.VMEM((G, N, LANES), F32),
                                                                      pltpu.VMEM((G, U * N, QL), F32)]) * n_buf,
        compiler_params=_params(("arbitrary", "arbitrary", "arbitrary")),
        name="wkv",
    )(*operands)


def _rwkv_layer(x, shift, wkv_state, prm, ln_g, ln_b, *, tm, wkv_cfg):
    mu, w_rkv, w1, w2, a1, a2, g1, g2, vec, r_k, w_o = prm
    B, T, D = x.shape
    w_rkv = jnp.stack([_fold_lanes(w_rkv[0]), _fold_lanes(w_rkv[1]), w_rkv[2]])
    vec_f = jnp.concatenate([_fold_lanes(vec[0:2]), vec[2:]], axis=0)
    r, k, v, d, a, g = _rwkv_proj(x, shift, mu, w_rkv, w1, _fold_lanes(w2), a1, _fold_lanes(a2), g1, g2, vec_f, tm=tm)
    vec8 = jnp.concatenate([_fold_lanes(vec[2:4]), vec[4:6], _fold_lanes(r_k.reshape(1, D)), jnp.zeros((3, D), F32)],
                           axis=0)
    y, s_new = _wkv(r, k, v, d, a, vec8, wkv_state, **wkv_cfg)
    x_new = _proj_norm(x.reshape(B * T, D), y.reshape(B * T, D), g.reshape(B * T, D), w_o,
                       jnp.zeros((D,), F32), ln_g, ln_b, tm=tm)
    return x_new.reshape(B, T, D), s_new, x[:, -1]


CONV_HIST = CONV_WIDTH - 1
CONV_HALO = 32


def _glu_body(x_ref, w_ref, b_ref, u_ref):
    h = _dot(x_ref[...], w_ref[...]) + b_ref[...]
    D = u_ref.shape[-1]
    u_ref[...] = h[:, :D] * _sigmoid(h[:, D:])


def _glu(x2, w, b, *, tm):
    M, D = x2.shape
    tm = min(tm, M)
    assert M % tm == 0
    return pl.pallas_call(
        _glu_body,
        grid=(M // tm,),
        in_specs=[pl.BlockSpec((tm, D), lambda i: (i, 0)),
                  pl.BlockSpec((D, 2 * D), lambda i: (0, 0)),
                  pl.BlockSpec((1, 2 * D), lambda i: (0, 0))],
        out_specs=pl.BlockSpec((tm, D), lambda i: (i, 0)),
        out_shape=jax.ShapeDtypeStruct((M, D), F32),
        compiler_params=_params(("arbitrary",)),
        name="conf_glu",
    )(x2, w, b.reshape(1, 2 * D))


def _conf_tail(conv, x, bdw_ref, cln_ref, w2_ref, b2_ref, g_ref, b_ref):
    cln = cln_ref[...]
    c = _silu(_layer_norm(conv + bdw_ref[...], cln[0:1], cln[1:2]))
    out = _dot(c, w2_ref[...]) + b2_ref[...]
    return _layer_norm(DN_ALPHA * x + out, g_ref[...], b_ref[...])


def _conf_long_body(seq_len, tm, x_ref, u_ref, halo_ref, wdw_ref, bdw_ref, cln_ref, w2_ref, b2_ref, g_ref, b_ref,
                    y_ref, ext_ref):
    i = pl.program_id(0)
    starts_sequence = (i * tm) % seq_len == 0
    ext_ref[0:CONV_HALO, :] = jnp.where(starts_sequence, 0.0, halo_ref[...])
    ext_ref[CONV_HALO:CONV_HALO + tm, :] = u_ref[...]
    ext_ref[CONV_HALO + tm:CONV_HALO + tm + SUBLANES, :] = jnp.zeros((SUBLANES, u_ref.shape[-1]), F32)
    wdw = wdw_ref[...]
    first = CONV_HALO - CONV_HIST
    groups = {}
    for j in range(CONV_WIDTH):
        shift, base = (first + j) % SUBLANES, (first + j) // SUBLANES * SUBLANES
        term = ext_ref[base:base + tm + SUBLANES, :] * wdw[j:j + 1]
        groups[shift] = term if shift not in groups else groups[shift] + term
    conv = functools.reduce(jnp.add, [acc[shift:shift + tm] for shift, acc in sorted(groups.items())])
    y_ref[...] = _conf_tail(conv, x_ref[...], bdw_ref, cln_ref, w2_ref, b2_ref, g_ref, b_ref)


def _conf_short_body(seq_len, nb, x_ref, u_ref, hist_ref, wdw_ref, bdw_ref, cln_ref, w2_ref, b2_ref, g_ref, b_ref,
                     y_ref, hist_out_ref, ext_ref):
    D = u_ref.shape[-1]
    ext_ref[:, 0:CONV_HIST, :] = hist_ref[...]
    ext_ref[:, CONV_HIST:CONV_HIST + seq_len, :] = u_ref[...].reshape(nb, seq_len, D)
    wdw = wdw_ref[...]
    conv = ext_ref[:, 0:seq_len, :] * wdw[0:1]
    for j in range(1, CONV_WIDTH):
        conv = conv + ext_ref[:, j:j + seq_len, :] * wdw[j:j + 1]
    hist_out_ref[...] = ext_ref[:, seq_len:seq_len + CONV_HIST, :]
    y_ref[...] = _conf_tail(conv.reshape(nb * seq_len, D), x_ref[...], bdw_ref, cln_ref, w2_ref, b2_ref, g_ref, b_ref)


def _conformer_layer(x, hist, prm, ln_g, ln_b, *, tm):
    w_pw1, b_pw1, w_dw, b_dw, cln, w_pw2, b_pw2 = prm
    B, T, D = x.shape
    M = B * T
    x2 = x.reshape(M, D)
    u = _glu(x2, w_pw1, b_pw1, tm=tm)
    const = lambda r, c: pl.BlockSpec((r, c), lambda i: (0, 0))
    weights = [w_dw, b_dw.reshape(1, D), cln, w_pw2, b_pw2.reshape(1, D), ln_g.reshape(1, D), ln_b.reshape(1, D)]
    wspecs = [const(CONV_WIDTH, D), const(1, D), const(2, D), const(D, D), const(1, D), const(1, D), const(1, D)]
    if hist is None:
        tm = min(tm, T)
        assert T % tm == 0 and tm % CONV_HALO == 0
        row = pl.BlockSpec((tm, D), lambda i: (i, 0))
        halo_spec = pl.BlockSpec((CONV_HALO, D), lambda i: (jnp.maximum(i * (tm // CONV_HALO) - 1, 0), 0))
        y = pl.pallas_call(
            functools.partial(_conf_long_body, T, tm),
            grid=(M // tm,),
            in_specs=[row, row, halo_spec] + wspecs,
            out_specs=row,
            out_shape=jax.ShapeDtypeStruct((M, D), F32),
            scratch_shapes=[pltpu.VMEM((CONV_HALO + tm + SUBLANES, D), F32)],
            compiler_params=_params(("arbitrary",)),
            name="conf_conv_long",
        )(x2, u, u, *weights)
        new_hist = u.reshape(B, T, D)[:, T - CONV_HIST:, :]
    else:
        assert T == SUBLANES
        nb = min(tm // T, B)
        assert B % nb == 0
        row = pl.BlockSpec((nb * T, D), lambda i: (i, 0))
        hspec = pl.BlockSpec((nb, CONV_HIST, D), lambda i: (i, 0, 0))
        y, new_hist = pl.pallas_call(
            functools.partial(_conf_short_body, T, nb),
            grid=(B // nb,),
            in_specs=[row, row, hspec] + wspecs,
            out_specs=[row, hspec],
            out_shape=[jax.ShapeDtypeStruct((M, D), F32), jax.ShapeDtypeStruct((B, CONV_HIST, D), F32)],
            scratch_shapes=[pltpu.VMEM((nb, CONV_HIST + T + 2, D), F32)],
            compiler_params=_params(("arbitrary",)),
            name="conf_conv_short",
        )(x2, u, hist, *weights)
    return y.reshape(B, T, D), new_hist


def _rms_norm(z, g):
    return z * lax.rsqrt(jnp.mean(z * z, axis=-1, keepdims=True) + RMS_EPS) * g


def _mla_proj_body(keys_bf16, x_ref, cos_ref, sin_ref, wdq_ref, qn_ref, wqn_ref, wqr_ref, wqs_ref, wckv_ref, wkr_ref,
                   wks_ref, kvn_ref, wuk_ref, qlat_ref, qpe_ref, ckv_ref, kpe_ref, *key_copies):
    x = x_ref[...]
    cos, sin = cos_ref[...], sin_ref[...]
    cq = _rms_norm(_dot(x, wdq_ref[...]), qn_ref[...])
    q_nope = _dot(cq, wqn_ref[...])
    q_rope = _dot(cq, wqr_ref[...])
    q_swap = _dot(cq, wqs_ref[...])
    for h in range(MLA_HEADS):
        q_lat = _dot(q_nope[:, h * QK_NOPE:(h + 1) * QK_NOPE], wuk_ref[h])
        qlat_ref[h] = (q_lat * MLA_SCALE).astype(qlat_ref.dtype)
        sl = slice(h * QK_ROPE, (h + 1) * QK_ROPE)
        qpe_ref[h] = ((q_rope[:, sl] * cos + q_swap[:, sl] * sin) * MLA_SCALE).astype(qpe_ref.dtype)
    ckv = _rms_norm(_dot(x, wckv_ref[...]), kvn_ref[...])
    kpe = _dot(x, wkr_ref[...]) * cos + _dot(x, wks_ref[...]) * sin
    ckv_ref[...] = ckv
    kpe_ref[...] = kpe
    if keys_bf16:
        key_copies[0][...] = ckv.astype(BF16)
        key_copies[1][...] = kpe.astype(BF16)


def _swap_halves(w):
    half = w.shape[-1] // 2
    return jnp.concatenate([w[..., half:], w[..., :half]], axis=-1)


def _rope_tables(pos):
    half = QK_ROPE // 2
    inv = ROPE_THETA ** (-jnp.arange(half, dtype=F32) / half)
    ang = pos[:, None] * inv[None, :]
    cos, sin = jnp.cos(ang), jnp.sin(ang)
    return jnp.concatenate([cos, cos], -1), jnp.concatenate([-sin, sin], -1)


def _mla_proj(x2, pos, w_dq, q_norm, w_uq, w_dkv, kv_norm, w_uk, *, tm, keys_bf16):
    M, D = x2.shape
    H = MLA_HEADS
    tm = min(tm, M)
    assert M % tm == 0
    cos, sin = _rope_tables(pos)
    w_qn = w_uq[:, :, :QK_NOPE].reshape(Q_LORA, H * QK_NOPE)
    w_qr = w_uq[:, :, QK_NOPE:].reshape(Q_LORA, H * QK_ROPE)
    w_qs = _swap_halves(w_uq[:, :, QK_NOPE:]).reshape(Q_LORA, H * QK_ROPE)
    w_ckv, w_kr = w_dkv[:, :KV_LORA], w_dkv[:, KV_LORA:]
    w_ks = _swap_halves(w_kr)
    w_ukT = jnp.transpose(w_uk, (1, 2, 0))
    full = lambda arr: pl.BlockSpec(arr.shape, lambda i: (0,) * arr.ndim)
    row = lambda c: pl.BlockSpec((tm, c), lambda i: (i, 0))
    hrow = lambda c: pl.BlockSpec((H, tm, c), lambda i: (0, i, 0))
    weights = [w_dq, q_norm.reshape(1, Q_LORA), w_qn, w_qr, w_qs, w_ckv, w_kr, w_ks, kv_norm.reshape(1, KV_LORA), w_ukT]
    qdt = BF16 if keys_bf16 else F32
    out_specs = [hrow(KV_LORA), hrow(QK_ROPE), row(KV_LORA), row(QK_ROPE)]
    out_shape = [jax.ShapeDtypeStruct((H, M, KV_LORA), qdt), jax.ShapeDtypeStruct((H, M, QK_ROPE), qdt),
                 jax.ShapeDtypeStruct((M, KV_LORA), F32), jax.ShapeDtypeStruct((M, QK_ROPE), F32)]
    if keys_bf16:
        out_specs += [row(KV_LORA), row(QK_ROPE)]
        out_shape += [jax.ShapeDtypeStruct((M, KV_LORA), BF16), jax.ShapeDtypeStruct((M, QK_ROPE), BF16)]
    return pl.pallas_call(
        functools.partial(_mla_proj_body, keys_bf16),
        grid=(M // tm,),
        in_specs=[row(D), row(QK_ROPE), row(QK_ROPE)] + [full(w) for w in weights],
        out_specs=out_specs,
        out_shape=out_shape,
        compiler_params=_params(("arbitrary",)),
        name="mla_proj",
    )(x2, cos, sin, *weights)


def _scores(ql, qp, ckv, kpe):
    return _dot_t(ql, ckv) + _dot_t(qp, kpe)


def _softmax_step(s, values, m_ref, l_ref, acc_ref):
    m_old = m_ref[...]
    m_new = jnp.maximum(m_old, jnp.max(s, axis=-1, keepdims=True))
    p = jnp.exp(s - m_new)
    alpha = jnp.exp(m_old - m_new)
    l_ref[...] = alpha * l_ref[...] + jnp.sum(p, axis=-1, keepdims=True)
    pv = functools.reduce(jnp.add, [_dot(p[:, ks], v) for ks, v in values])
    acc_ref[...] = alpha * acc_ref[...] + pv
    m_ref[...] = m_new


HEAD_GROUP = 1


def _flash_body(tq, tk, qi_ref, kj_ref, qlat_ref, qpe_ref, ckv_ref, kpe_ref, o_ref, m_ref, l_ref, acc_ref):
    n = pl.program_id(1)
    i, j = qi_ref[n], kj_ref[n]
    H = MLA_HEADS
    last_j = ((i + 1) * tq - 1) // tk

    @pl.when(j == 0)
    def _():
        m_ref[...] = jnp.full(m_ref.shape, -jnp.inf, F32)
        l_ref[...] = jnp.zeros(l_ref.shape, F32)
        acc_ref[...] = jnp.zeros(acc_ref.shape, F32)

    def update(masked):
        ckv, kpe = ckv_ref[...], kpe_ref[...]

        def group_scores(h0):
            hs = slice(h0, h0 + HEAD_GROUP)
            return _scores(qlat_ref[hs].reshape(HEAD_GROUP * tq, KV_LORA), qpe_ref[hs].reshape(HEAD_GROUP * tq, QK_ROPE),
                           ckv, kpe)

        starts = list(range(0, H, HEAD_GROUP))
        s_next = group_scores(starts[0])
        for g, h0 in enumerate(starts):
            s = s_next
            if g + 1 < len(starts):
                s_next = group_scores(starts[g + 1])
            rows = slice(h0 * tq, (h0 + HEAD_GROUP) * tq)
            if masked:
                q_pos = i * tq + lax.broadcasted_iota(jnp.int32, (HEAD_GROUP * tq, 1), 0) % tq
                k_pos = j * tk + lax.broadcasted_iota(jnp.int32, (1, tk), 1)
                s = jnp.where(k_pos <= q_pos, s, -jnp.inf)
            _softmax_step(s, [(slice(None), ckv)], m_ref.at[rows], l_ref.at[rows], acc_ref.at[rows])

    @pl.when(j < last_j)
    def _():
        update(False)

    @pl.when(j == last_j)
    def _():
        update(True)
        o_ref[...] = (acc_ref[...] / l_ref[...]).reshape(H, tq, KV_LORA)


def _flash_attention(q_lat, q_pe, ckv, kpe, B, T, *, tq, tk):
    H = MLA_HEADS
    tq, tk = min(tq, T), min(tk, T)
    assert T % tq == 0 and T % tk == 0 and tk % tq == 0
    nq, nk = T // tq, T // tk
    pairs = [(i, j) for i in range(nq) for j in range(((i + 1) * tq - 1) // tk + 1)]
    qi = jnp.asarray([p[0] for p in pairs], jnp.int32)
    kj = jnp.asarray([p[1] for p in pairs], jnp.int32)
    qmap = lambda b, n, qi, kj: (0, b * nq + qi[n], 0)
    kmap = lambda b, n, qi, kj: (b * nk + kj[n], 0)
    grid_spec = pltpu.PrefetchScalarGridSpec(
        num_scalar_prefetch=2,
        grid=(B, len(pairs)),
        in_specs=[pl.BlockSpec((H, tq, KV_LORA), qmap), pl.BlockSpec((H, tq, QK_ROPE), qmap),
                  pl.BlockSpec((tk, KV_LORA), kmap), pl.BlockSpec((tk, QK_ROPE), kmap)],
        out_specs=pl.BlockSpec((H, tq, KV_LORA), qmap),
        scratch_shapes=[pltpu.VMEM((H * tq, 1), F32), pltpu.VMEM((H * tq, 1), F32), pltpu.VMEM((H * tq, KV_LORA), F32)],
    )
    return pl.pallas_call(
        functools.partial(_flash_body, tq, tk),
        grid_spec=grid_spec,
        out_shape=jax.ShapeDtypeStruct((H, B * T, KV_LORA), F32),
        compiler_params=_params(("arbitrary", "arbitrary")),
        name="mla_flash",
    )(qi, kj, q_lat, q_pe, ckv, kpe)


def _paged_body(PG, T, page, pt_ref, qlat_ref, qpe_ref, ckvn_ref, kpen_ref, *rest):
    pages_ckv, pages_kpe = rest[:PG], rest[PG:2 * PG]
    o_ref, m_ref, l_ref, acc_ref, selfk_ref, selfp_ref = rest[2 * PG:]
    g = pl.program_id(1)
    H = MLA_HEADS
    ql = qlat_ref[...].reshape(H * T, KV_LORA).astype(BF16)
    qp = qpe_ref[...].reshape(H * T, QK_ROPE).astype(BF16)

    @pl.when(g == 0)
    def _():
        m_ref[...] = jnp.full(m_ref.shape, -jnp.inf, F32)
        l_ref[...] = jnp.zeros(l_ref.shape, F32)
        acc_ref[...] = jnp.zeros(acc_ref.shape, F32)

    halves = [range(0, PG // 2), range(PG // 2, PG)] if PG > 1 else [range(PG)]
    keys = [jnp.concatenate([pages_ckv[p][0, 0].astype(BF16) for p in half], axis=0) for half in halves]
    rope_keys_t = [jnp.concatenate([pages_kpe[p][0, 0].astype(BF16) for p in half], axis=1) for half in halves]
    scores = [_dot_t(ql, k) + _dot(qp, kt) for k, kt in zip(keys, rope_keys_t)]
    for s, k in zip(scores, keys):
        _softmax_step(s, [(slice(None), k)], m_ref, l_ref, acc_ref)

    @pl.when(g == pl.num_programs(1) - 1)
    def _():
        selfk_ref[...] = jnp.zeros(selfk_ref.shape, F32)
        selfp_ref[...] = jnp.zeros(selfp_ref.shape, F32)
        selfk_ref[0:T, :] = ckvn_ref[...]
        selfp_ref[0:T, :] = kpen_ref[...]
        ckv = selfk_ref[...]
        s = _scores(ql, qp, ckv, selfp_ref[...])
        q_pos = lax.broadcasted_iota(jnp.int32, (H * T, 1), 0) % T
        k_pos = lax.broadcasted_iota(jnp.int32, (1, page), 1)
        s = jnp.where(k_pos <= q_pos, s, -jnp.inf)
        _softmax_step(s, [(slice(None), ckv)], m_ref, l_ref, acc_ref)
        o_ref[...] = (acc_ref[...] / l_ref[...]).reshape(H, T, KV_LORA)


def _paged_attention(q_lat, q_pe, ckv, kpe, cache_ckv, cache_kpe_t, page_table, layer, T, *, PG):
    H = MLA_HEADS
    DB, n_pages = page_table.shape
    page = cache_ckv.shape[2]
    assert n_pages % PG == 0 and T <= page
    qmap = lambda b, g, pt: (0, b, 0)
    nmap = lambda b, g, pt: (b, 0)
    pmap = lambda p: (lambda b, g, pt: (layer, pt[b * n_pages + g * PG + p], 0, 0))
    grid_spec = pltpu.PrefetchScalarGridSpec(
        num_scalar_prefetch=1,
        grid=(DB, n_pages // PG),
        in_specs=[pl.BlockSpec((H, T, KV_LORA), qmap), pl.BlockSpec((H, T, QK_ROPE), qmap),
                  pl.BlockSpec((T, KV_LORA), nmap), pl.BlockSpec((T, QK_ROPE), nmap)]
                 + [pl.BlockSpec((1, 1, page, KV_LORA), pmap(p)) for p in range(PG)]
                 + [pl.BlockSpec((1, 1, QK_ROPE, page), pmap(p)) for p in range(PG)],
        out_specs=pl.BlockSpec((H, T, KV_LORA), qmap),
        scratch_shapes=[pltpu.VMEM((H * T, 1), F32), pltpu.VMEM((H * T, 1), F32), pltpu.VMEM((H * T, KV_LORA), F32),
                        pltpu.VMEM((page, KV_LORA), F32), pltpu.VMEM((page, QK_ROPE), F32)],
    )
    return pl.pallas_call(
        functools.partial(_paged_body, PG, T, page),
        grid_spec=grid_spec,
        out_shape=jax.ShapeDtypeStruct((H, DB * T, KV_LORA), F32),
        compiler_params=_params(("arbitrary", "arbitrary")),
        name="mla_paged",
    )(page_table.reshape(-1), q_lat, q_pe, ckv, kpe, *([cache_ckv] * PG), *([cache_kpe_t] * PG))


def _mla_out_body(x_ref, o_ref, wuv_ref, wo_ref, g_ref, b_ref, y_ref):
    heads = [_dot(o_ref[h], wuv_ref[h]) for h in range(MLA_HEADS)]
    out = _dot(jnp.concatenate(heads, axis=-1), wo_ref[...])
    y_ref[...] = _layer_norm(DN_ALPHA * x_ref[...] + out, g_ref[...], b_ref[...])


def _mla_out(x2, o_lat, w_uv, w_o, ln_g, ln_b, *, tm):
    M, D = x2.shape
    H = MLA_HEADS
    tm = min(tm, M)
    assert M % tm == 0
    w_uvh = jnp.transpose(w_uv, (1, 0, 2))
    const = lambda shape: pl.BlockSpec(shape, lambda i: (0,) * len(shape))
    return pl.pallas_call(
        _mla_out_body,
        grid=(M // tm,),
        in_specs=[pl.BlockSpec((tm, D), lambda i: (i, 0)), pl.BlockSpec((H, tm, KV_LORA), lambda i: (0, i, 0)),
                  const(w_uvh.shape), const(w_o.shape), const((1, D)), const((1, D))],
        out_specs=pl.BlockSpec((tm, D), lambda i: (i, 0)),
        out_shape=jax.ShapeDtypeStruct((M, D), F32),
        compiler_params=_params(("arbitrary",)),
        name="mla_out",
    )(x2, o_lat, w_uvh, w_o, ln_g.reshape(1, D), ln_b.reshape(1, D))


def _mla_layer(x, pos0, cache, prm, ln_g, ln_b, *, tm, attn_cfg):
    w_dq, q_norm, w_uq, w_dkv, kv_norm, w_uk, w_uv, w_o = prm
    B, T, D = x.shape
    x2 = x.reshape(B * T, D)
    pos = jnp.tile(pos0 + jnp.arange(T, dtype=F32), B)
    proj = _mla_proj(x2, pos, w_dq, q_norm, w_uq, w_dkv, kv_norm, w_uk, tm=tm, keys_bf16=cache is None)
    q_lat, q_pe, ckv, kpe = proj[:4]
    if cache is None:
        o_lat = _flash_attention(q_lat, q_pe, proj[4], proj[5], B, T, **attn_cfg)
    else:
        o_lat = _paged_attention(q_lat, q_pe, ckv, kpe, *cache, T, **attn_cfg)
    y = _mla_out(x2, o_lat, w_uv, w_o, ln_g, ln_b, tm=tm)
    return y.reshape(B, T, D), ckv.reshape(B, T, KV_LORA), kpe.reshape(B, T, QK_ROPE)


def kernel(x_prompt, x_sample, state_wkv, state_shift, state_conv, cache_mla_ckv, cache_mla_kpe, state_ffn_conv, page_table, ln_g, ln_b, rwkv_mu, rwkv_w_rkv, rwkv_w1, rwkv_w2, rwkv_a1, rwkv_a2, rwkv_g1, rwkv_g2, rwkv_vec, rwkv_r_k, rwkv_w_o, conf_w_pw1, conf_b_pw1, conf_w_dw, conf_b_dw, conf_ln, conf_w_pw2, conf_b_pw2, mla_w_dq, mla_q_norm, mla_w_uq, mla_w_dkv, mla_kv_norm, mla_w_uk, mla_w_uv, mla_w_o, ffn_w_in, ffn_w_dw, ffn_b_dw, ffn_w_out):
    bf = lambda w: w.astype(BF16)
    past_len = page_table.shape[1] * cache_mla_ckv.shape[2]
    xp, xs = x_prompt, x_sample
    out_p = dict(wkv=[], shift=[], conv=[], ckv=[], kpe=[], ffn=[])
    out_s = dict(wkv=[], shift=[], conv=[], ckv=[], kpe=[], ffn=[])
    for i in range(DEPTH):
        j, kind = i // N_MIXERS, i % N_MIXERS
        g0, b0, g1, b1 = ln_g[i, 0], ln_b[i, 0], ln_g[i, 1], ln_b[i, 1]
        if kind == 0:
            prm = (rwkv_mu[j], bf(rwkv_w_rkv[j]), bf(rwkv_w1[j]), bf(rwkv_w2[j]), bf(rwkv_a1[j]), bf(rwkv_a2[j]),
                   bf(rwkv_g1[j]), bf(rwkv_g2[j]), rwkv_vec[j], rwkv_r_k[j], bf(rwkv_w_o[j]))
            xp, st, sh = _rwkv_layer(xp, None, None, prm, g0, b0, tm=512, wkv_cfg=dict(Gb=2, Gq=4, Tc=256, U=16))
            out_p["wkv"].append(st)
            out_p["shift"].append(sh)
            xs, st, sh = _rwkv_layer(xs, state_shift[j], state_wkv[j], prm, g0, b0, tm=512,
                                     wkv_cfg=dict(Gb=2, Gq=4, Tc=8, U=8))
            out_s["wkv"].append(st)
            out_s["shift"].append(sh)
        elif kind == 1:
            prm = (bf(conf_w_pw1[j]), conf_b_pw1[j], conf_w_dw[j], conf_b_dw[j], conf_ln[j], bf(conf_w_pw2[j]),
                   conf_b_pw2[j])
            xp, cb = _conformer_layer(xp, None, prm, g0, b0, tm=256)
            out_p["conv"].append(cb)
            xs, cb = _conformer_layer(xs, state_conv[j], prm, g0, b0, tm=256)
            out_s["conv"].append(cb)
        else:
            prm = (bf(mla_w_dq[j]), mla_q_norm[j], bf(mla_w_uq[j]), bf(mla_w_dkv[j]), mla_kv_norm[j],
                   bf(mla_w_uk[j]), bf(mla_w_uv[j]), bf(mla_w_o[j]))
            xp, ckv, kpe = _mla_layer(xp, 0.0, None, prm, g0, b0, tm=512, attn_cfg=dict(tq=256, tk=1024))
            out_p["ckv"].append(ckv)
            out_p["kpe"].append(kpe)
            xs, ckv, kpe = _mla_layer(xs, float(past_len), (cache_mla_ckv, jnp.swapaxes(cache_mla_kpe, 2, 3), page_table, j), prm,
                                      g0, b0, tm=512, attn_cfg=dict(PG=32))
            out_s["ckv"].append(ckv)
            out_s["kpe"].append(kpe)
        w_in, w_out = bf(ffn_w_in[i]), bf(ffn_w_out[i])
        xp, fb = _conv_ffn(xp, None, w_in, ffn_w_dw[i], ffn_b_dw[i], w_out, g1, b1, tm=512, fc=1408)
        out_p["ffn"].append(fb)
        xs, fb = _conv_ffn(xs, state_ffn_conv[i], w_in, ffn_w_dw[i], ffn_b_dw[i], w_out, g1, b1, tm=512, fc=1408)
        out_s["ffn"].append(fb)
    names = ("wkv", "shift", "conv", "ckv", "kpe", "ffn")
    return (xp, xs) + tuple(jnp.stack(out_p[n]) for n in names) + tuple(jnp.stack(out_s[n]) for n in names)
```
